```python
import functools
import jax, jax.numpy as jnp
from jax import lax
import numpy as np

D_MODEL = 4096
BATCH = 1
SEQ = 8192
DEPTH = 1
DEC_BATCH = 32
DEC_SEQ = 1
PAST_LEN = 8192
PAGE_SIZE = 128

MIX_WIDTH = D_MODEL
RWKV_WIDTH = MIX_WIDTH // 2
RWKV_HEAD = 64
RWKV_HEADS = RWKV_WIDTH // RWKV_HEAD
DECAY_LORA = 128
AAA_LORA = 96
GATE_LORA = 256
RWKV_PROJ = 3 * RWKV_WIDTH + DECAY_LORA + AAA_LORA + GATE_LORA
RWKV_SPLITS = (RWKV_WIDTH, 2 * RWKV_WIDTH, 3 * RWKV_WIDTH,
               3 * RWKV_WIDTH + DECAY_LORA, 3 * RWKV_WIDTH + DECAY_LORA + AAA_LORA)
SB_WIDTH = MIX_WIDTH - RWKV_WIDTH
SB_HEAD = 128
SB_HEADS = SB_WIDTH // SB_HEAD
SB_BIAS_INIT = -8.0
IN_WIDTH = RWKV_PROJ + 3 * SB_WIDTH
IN_SPLITS = (RWKV_PROJ, RWKV_PROJ + SB_WIDTH, RWKV_PROJ + 2 * SB_WIDTH)
D_FF = 4 * D_MODEL
PLE_DIM = 256
Q_BLOCK = 128
RMS_EPS = 1e-6
GN_EPS = 64e-5
L2_EPS = 1e-12

kernel_name = "hymba_rwkv7_stickbreaking_decoder_step"


def rms_norm(x, g):
    xf = x.astype(jnp.float32)
    y = xf * lax.rsqrt(jnp.mean(xf * xf, axis=-1, keepdims=True) + RMS_EPS)
    return (y * g.astype(jnp.float32)).astype(x.dtype)


def rwkv7_mix(feats, shift_prev, wkv0, mu, w0, w2, a0, a2, g2, k_k, k_a, r_k, ln_w, ln_b):
    f32 = jnp.float32
    B, T, _ = feats.shape
    prev = jnp.concatenate([shift_prev[:, None, :].astype(feats.dtype), feats[:, :-1]], axis=1)
    xm = feats + (prev - feats) * mu
    r, k, v, xw, xa, xg = jnp.split(xm, RWKV_SPLITS, axis=-1)
    w_log = -jax.nn.softplus(-(w0 + jnp.tanh(xw) @ w2).astype(f32)) - 0.5
    decay = jnp.exp(-jnp.exp(w_log))
    a = jax.nn.sigmoid(a0 + xa @ a2)
    g = jax.nn.sigmoid(xg) @ g2

    def heads(t):
        return t.astype(f32).reshape(B, T, RWKV_HEADS, RWKV_HEAD)

    kk = heads(k * k_k)
    kk = kk / jnp.maximum(jnp.sqrt(jnp.sum(kk * kk, axis=-1, keepdims=True)), L2_EPS)
    k = k * (1 + (a - 1) * k_a)
    r_h, k_h, v_h, a_h, d_h = heads(r), heads(k), heads(v), heads(a), heads(decay)
    b_h = kk * a_h

    def step(S, inp):
        r_t, d_t, k_t, v_t, kk_t, b_t = inp
        sa = jnp.einsum("bhvk,bhk->bhv", S, -kk_t)
        S = (S * d_t[:, :, None, :] + sa[..., None] * b_t[:, :, None, :]
             + v_t[..., None] * k_t[:, :, None, :])
        return S, jnp.einsum("bhvk,bhk->bhv", S, r_t)

    xs = tuple(jnp.moveaxis(t, 1, 0) for t in (r_h, d_h, k_h, v_h, kk, b_h))
    S_fin, ys = lax.scan(step, wkv0.astype(f32), xs)
    y = jnp.moveaxis(ys, 0, 1)
    yc = y - jnp.mean(y, axis=-1, keepdims=True)
    y = yc * lax.rsqrt(jnp.mean(yc * yc, axis=-1, keepdims=True) + GN_EPS)
    y = y.reshape(B, T, RWKV_WIDTH) * ln_w + ln_b
    bonus = (jnp.sum(r_h * k_h * r_k, axis=-1, keepdims=True) * v_h).reshape(B, T, RWKV_WIDTH)
    out = (y + bonus) * g
    return out.astype(feats.dtype), feats[:, -1], S_fin


def stick_breaking(q, k, v, bias, q_pos, k_pos):
    f32 = jnp.float32
    z = (jnp.einsum("bqhd,bkhd->bhqk", q.astype(f32), k.astype(f32)) * (SB_HEAD ** -0.5)
         + bias.astype(f32)[None, :, None, None])
    causal = k_pos[None, :] < q_pos[:, None]
    log_stay = jnp.where(causal, jax.nn.log_sigmoid(-z), 0.0)
    after = lax.cumsum(log_stay, axis=3, reverse=True) - log_stay
    weights = jnp.where(causal, jnp.exp(jax.nn.log_sigmoid(z) + after), 0.0)
    return jnp.einsum("bhqk,bkhd->bqhd", weights, v.astype(f32))


def sb_prompt(q, k, v, bias):
    B, S, H, Dh = q.shape
    k_pos = jnp.arange(S)

    def block(i):
        start = i * Q_BLOCK
        qb = lax.dynamic_slice_in_dim(q, start, Q_BLOCK, axis=1)
        return stick_breaking(qb, k, v, bias, start + jnp.arange(Q_BLOCK), k_pos)

    out = lax.map(block, jnp.arange(S // Q_BLOCK))
    return jnp.moveaxis(out, 0, 1).reshape(B, S, H, Dh)


def sb_sample(q, k, v, bias, past_k, past_v):
    past_len = past_k.shape[1]
    T = q.shape[1]
    k_all = jnp.concatenate([past_k.astype(jnp.float32), k.astype(jnp.float32)], axis=1)
    v_all = jnp.concatenate([past_v.astype(jnp.float32), v.astype(jnp.float32)], axis=1)
    return stick_breaking(q, k_all, v_all, bias, past_len + jnp.arange(T),
                          jnp.arange(past_len + T))


def hybrid_layer(x, pe, shift_prev, wkv0, attend, lw):
    B, T, _ = x.shape
    h = rms_norm(x, lw["norm_mix_pre"])
    proj = h @ lw["w_in"]
    feats, q, k, v = jnp.split(proj, IN_SPLITS, axis=-1)
    r_out, shift_new, wkv_new = rwkv7_mix(
        feats, shift_prev, wkv0, lw["rwkv_mu"], lw["rwkv_w0"], lw["rwkv_w2"], lw["rwkv_a0"],
        lw["rwkv_a2"], lw["rwkv_g2"], lw["rwkv_k_k"], lw["rwkv_k_a"], lw["rwkv_r_k"],
        lw["rwkv_ln_w"], lw["rwkv_ln_b"])
    qh = q.reshape(B, T, SB_HEADS, SB_HEAD)
    kh = k.reshape(B, T, SB_HEADS, SB_HEAD)
    vh = v.reshape(B, T, SB_HEADS, SB_HEAD)
    a_out = rms_norm(attend(qh, kh, vh, lw["sb_bias"]), lw["sb_norm"])
    a_out = a_out.reshape(B, T, SB_WIDTH).astype(x.dtype)
    mix = jnp.concatenate([r_out, a_out], axis=-1) @ lw["w_out"]
    x = x + rms_norm(mix, lw["norm_mix_post"])
    f = rms_norm(x, lw["norm_ffn_pre"]) @ lw["w_up"]
    f = jnp.square(jax.nn.relu(f)) @ lw["w_down"]
    x = x + rms_norm(f, lw["norm_ffn_post"])
    x = x + jax.nn.sigmoid(x @ lw["w_ple_gate"]) * (pe @ lw["w_ple_proj"])
    return x, shift_new, wkv_new, kh, vh


def setup_inputs(seed: int = 0) -> dict:
    key = jax.random.key(seed)
    ks = iter(jax.random.split(key, 40))
    nrm = lambda shape, s=1.0: s * jax.random.normal(next(ks), shape, jnp.float32)
    gain = lambda shape: 1.0 + 0.05 * jax.random.normal(next(ks), shape, jnp.float32)
    n_pages = PAST_LEN // PAGE_SIZE
    n_used = DEC_BATCH * n_pages
    n_phys = (n_used * 5 + 3) // 4
    perm = jax.random.permutation(next(ks), n_phys)
    page_table = perm[:n_used].reshape(DEC_BATCH, n_pages).astype(jnp.int32)
    return {
        "x_prompt": nrm((BATCH, SEQ, D_MODEL)),
        "x_sample": nrm((DEC_BATCH, DEC_SEQ, D_MODEL)),
        "p_prompt": nrm((DEPTH, BATCH, SEQ, PLE_DIM)),
        "p_sample": nrm((DEPTH, DEC_BATCH, DEC_SEQ, PLE_DIM)),
        "state_rwkv_shift": nrm((DEPTH, DEC_BATCH, RWKV_PROJ)),
        "state_rwkv_wkv": nrm((DEPTH, DEC_BATCH, RWKV_HEADS, RWKV_HEAD, RWKV_HEAD)),
        "cache_k": nrm((DEPTH, n_phys, PAGE_SIZE, SB_HEADS, SB_HEAD)),
        "cache_v": nrm((DEPTH, n_phys, PAGE_SIZE, SB_HEADS, SB_HEAD)),
        "page_table": page_table,
        "norm_mix_pre": gain((DEPTH, D_MODEL)),
        "norm_mix_post": gain((DEPTH, D_MODEL)),
        "norm_ffn_pre": gain((DEPTH, D_MODEL)),
        "norm_ffn_post": gain((DEPTH, D_MODEL)),
        "w_in": nrm((DEPTH, D_MODEL, IN_WIDTH), D_MODEL ** -0.5),
        "rwkv_mu": jax.random.uniform(next(ks), (DEPTH, RWKV_PROJ), jnp.float32),
        "rwkv_w0": nrm((DEPTH, RWKV_WIDTH), 0.5),
        "rwkv_w2": nrm((DEPTH, DECAY_LORA, RWKV_WIDTH), DECAY_LORA ** -0.5),
        "rwkv_a0": nrm((DEPTH, RWKV_WIDTH), 0.5),
        "rwkv_a2": nrm((DEPTH, AAA_LORA, RWKV_WIDTH), AAA_LORA ** -0.5),
        "rwkv_g2": nrm((DEPTH, GATE_LORA, RWKV_WIDTH), GATE_LORA ** -0.5),
        "rwkv_k_k": gain((DEPTH, RWKV_WIDTH)),
        "rwkv_k_a": gain((DEPTH, RWKV_WIDTH)),
        "rwkv_r_k": nrm((DEPTH, RWKV_HEADS, RWKV_HEAD), 0.1),
        "rwkv_ln_w": gain((DEPTH, RWKV_WIDTH)),
        "rwkv_ln_b": nrm((DEPTH, RWKV_WIDTH), 0.01),
        "sb_norm": gain((DEPTH, SB_HEADS, SB_HEAD)),
        "sb_bias": SB_BIAS_INIT + nrm((DEPTH, SB_HEADS), 0.1),
        "w_out": nrm((DEPTH, MIX_WIDTH, D_MODEL), MIX_WIDTH ** -0.5),
        "w_up": nrm((DEPTH, D_MODEL, D_FF), D_MODEL ** -0.5),
        "w_down": nrm((DEPTH, D_FF, D_MODEL), D_FF ** -0.5),
        "w_ple_gate": nrm((DEPTH, D_MODEL, D_MODEL), D_MODEL ** -0.5),
        "w_ple_proj": nrm((DEPTH, PLE_DIM, D_MODEL), PLE_DIM ** -0.5),
    }


def reference(x_prompt, x_sample, p_prompt, p_sample, state_rwkv_shift, state_rwkv_wkv,
              cache_k, cache_v, page_table, norm_mix_pre, norm_mix_post, norm_ffn_pre,
              norm_ffn_post, w_in, rwkv_mu, rwkv_w0, rwkv_w2, rwkv_a0, rwkv_a2, rwkv_g2,
              rwkv_k_k, rwkv_k_a, rwkv_r_k, rwkv_ln_w, rwkv_ln_b, sb_norm, sb_bias, w_out,
              w_up, w_down, w_ple_gate, w_ple_proj):
    B = x_prompt.shape[0]
    DB = x_sample.shape[0]
    n_pages = page_table.shape[1]
    past_len = n_pages * cache_k.shape[2]
    yp, ys = x_prompt, x_sample
    shift_p, shift_s, wkv_p, wkv_s, kp_l, vp_l, ks_l, vs_l = [], [], [], [], [], [], [], []
    for i in range(DEPTH):
        lw = dict(norm_mix_pre=norm_mix_pre[i], norm_mix_post=norm_mix_post[i],
                  norm_ffn_pre=norm_ffn_pre[i], norm_ffn_post=norm_ffn_post[i], w_in=w_in[i],
                  rwkv_mu=rwkv_mu[i], rwkv_w0=rwkv_w0[i], rwkv_w2=rwkv_w2[i],
                  rwkv_a0=rwkv_a0[i], rwkv_a2=rwkv_a2[i], rwkv_g2=rwkv_g2[i],
                  rwkv_k_k=rwkv_k_k[i], rwkv_k_a=rwkv_k_a[i], rwkv_r_k=rwkv_r_k[i],
                  rwkv_ln_w=rwkv_ln_w[i], rwkv_ln_b=rwkv_ln_b[i], sb_norm=sb_norm[i],
                  sb_bias=sb_bias[i], w_out=w_out[i], w_up=w_up[i], w_down=w_down[i],
                  w_ple_gate=w_ple_gate[i], w_ple_proj=w_ple_proj[i])
        zero_shift = jnp.zeros((B, RWKV_PROJ), x_prompt.dtype)
        zero_wkv = jnp.zeros((B, RWKV_HEADS, RWKV_HEAD, RWKV_HEAD), jnp.float32)
        yp, sp, wp, kp, vp = hybrid_layer(yp, p_prompt[i], zero_shift, zero_wkv, sb_prompt, lw)
        past_k = cache_k[i][page_table].reshape(DB, past_len, SB_HEADS, SB_HEAD)
        past_v = cache_v[i][page_table].reshape(DB, past_len, SB_HEADS, SB_HEAD)
        attend_s = functools.partial(sb_sample, past_k=past_k, past_v=past_v)
        ys, ss, wsn, kn, vn = hybrid_layer(ys, p_sample[i], state_rwkv_shift[i],
                                           state_rwkv_wkv[i], attend_s, lw)
        shift_p.append(sp); shift_s.append(ss); wkv_p.append(wp); wkv_s.append(wsn)
        kp_l.append(kp); vp_l.append(vp); ks_l.append(kn); vs_l.append(vn)
    return (yp, ys, jnp.stack(shift_p), jnp.stack(shift_s), jnp.stack(wkv_p), jnp.stack(wkv_s),
            jnp.stack(kp_l), jnp.stack(vp_l), jnp.stack(ks_l), jnp.stack(vs_l))
```

```python
import functools

import jax
import jax.numpy as jnp
import numpy as np
from jax import lax
from jax.experimental import pallas as pl
from jax.experimental.pallas import tpu as pltpu

F32 = jnp.float32
BF16 = jnp.bfloat16

RMS_EPS = 1e-6
GN_EPS = 64e-5
L2_EPS = 1e-12

LANES = 128
RWKV_HEAD = 64
SB_HEAD = 128
DECAY_LORA = 128
AAA_LORA = 96
GATE_LORA = 256
LORA_PAD = 512
VMEM_LIMIT = 52 * 1024 * 1024


def _cparams(sem):
    return pltpu.CompilerParams(dimension_semantics=sem, vmem_limit_bytes=VMEM_LIMIT)


def _tile(n, pref):
    if n <= pref:
        return n
    t = pref
    while n % t:
        t //= 2
    return t


def _mm_body(a_ref, b_ref, o_ref, acc_ref, *, nk, relu2):
    part = jnp.dot(a_ref[...], b_ref[...], preferred_element_type=F32)

    def finish(acc):
        if relu2:
            acc = jnp.square(jnp.maximum(acc, 0.0))
        o_ref[...] = acc.astype(o_ref.dtype)

    if nk == 1:
        finish(part)
    else:
        k = pl.program_id(2)

        @pl.when(k == 0)
        def _():
            acc_ref[...] = part

        @pl.when(k > 0)
        def _():
            acc_ref[...] += part

        @pl.when(k == nk - 1)
        def _():
            finish(acc_ref[...])


def _matmul(a, b, *, out_dtype=F32, relu2=False, tm=1024, tn=1024, tk=2048, name="mm"):
    m, kdim = a.shape
    n = b.shape[1]
    tm, tn = _tile(m, tm), _tile(n, tn)
    tk = kdim if kdim <= 4096 else _tile(kdim, tk)
    nk = kdim // tk
    return pl.pallas_call(
        functools.partial(_mm_body, nk=nk, relu2=relu2),
        grid=(n // tn, m // tm, nk),
        in_specs=[pl.BlockSpec((tm, tk), lambda j, i, k: (i, k)),
                  pl.BlockSpec((tk, tn), lambda j, i, k: (k, j))],
        out_specs=pl.BlockSpec((tm, tn), lambda j, i, k: (i, j)),
        out_shape=jax.ShapeDtypeStruct((m, n), out_dtype),
        scratch_shapes=[pltpu.VMEM((tm, tn) if nk > 1 else (8, LANES), F32)],
        compiler_params=_cparams(("parallel", "parallel", "arbitrary")),
        name=name,
    )(a, b)


def _mm2_body(a1_ref, a2_ref, b1_ref, b2_ref, o_ref):
    o_ref[...] = (jnp.dot(a1_ref[...], b1_ref[...], preferred_element_type=F32)
                  + jnp.dot(a2_ref[...], b2_ref[...], preferred_element_type=F32))


def _matmul_cat(a1, a2, b, *, tm=1024, tn=1024):
    m, k1 = a1.shape
    k2 = a2.shape[1]
    n = b.shape[1]
    tm, tn = _tile(m, tm), _tile(n, tn)
    nb1 = k1 // k2
    assert k1 == nb1 * k2
    return pl.pallas_call(
        _mm2_body,
        grid=(n // tn, m // tm),
        in_specs=[pl.BlockSpec((tm, k1), lambda j, i: (i, 0)),
                  pl.BlockSpec((tm, k2), lambda j, i: (i, 0)),
                  pl.BlockSpec((k1, tn), lambda j, i: (0, j)),
                  pl.BlockSpec((k2, tn), lambda j, i: (nb1, j))],
        out_specs=pl.BlockSpec((tm, tn), lambda j, i: (i, j)),
        out_shape=jax.ShapeDtypeStruct((m, n), F32),
        compiler_params=_cparams(("parallel", "parallel")),
        name="mm_out",
    )(a1, a2, b, b)


def _ple_body(a_ref, b_ref, x_ref, pe_ref, wp_ref, o_ref):
    gate = jnp.dot(a_ref[...], b_ref[...], preferred_element_type=F32)
    proj = jnp.dot(pe_ref[...], wp_ref[...], preferred_element_type=F32)
    o_ref[...] = x_ref[...] + jax.nn.sigmoid(gate) * proj


def _ple(xb, wg, x, pe, wp, *, tm=512, tn=1024):
    m, kdim = xb.shape
    n = wg.shape[1]
    kp = pe.shape[1]
    tm, tn = _tile(m, tm), _tile(n, tn)
    return pl.pallas_call(
        _ple_body,
        grid=(n // tn, m // tm),
        in_specs=[pl.BlockSpec((tm, kdim), lambda j, i: (i, 0)),
                  pl.BlockSpec((kdim, tn), lambda j, i: (0, j)),
                  pl.BlockSpec((tm, tn), lambda j, i: (i, j)),
                  pl.BlockSpec((tm, kp), lambda j, i: (i, 0)),
                  pl.BlockSpec((kp, tn), lambda j, i: (0, j))],
        out_specs=pl.BlockSpec((tm, tn), lambda j, i: (i, j)),
        out_shape=jax.ShapeDtypeStruct((m, n), F32),
        compiler_params=_cparams(("parallel", "parallel")),
        name="ple",
    )(xb, wg, x, pe, wp)


def _rms(x, g):
    return x * lax.rsqrt(jnp.mean(x * x, axis=-1, keepdims=True) + RMS_EPS) * g


def _norm_cast_body(x_ref, g_ref, o_ref):
    o_ref[...] = _rms(x_ref[...], g_ref[...]).astype(o_ref.dtype)


def _norm_cast(x, g, *, tm=256):
    m, d = x.shape
    tm = _tile(m, tm)
    return pl.pallas_call(
        _norm_cast_body,
        grid=(m // tm,),
        in_specs=[pl.BlockSpec((tm, d), lambda i: (i, 0)),
                  pl.BlockSpec((1, d), lambda i: (0, 0))],
        out_specs=pl.BlockSpec((tm, d), lambda i: (i, 0)),
        out_shape=jax.ShapeDtypeStruct((m, d), BF16),
        compiler_params=_cparams(("parallel",)),
        name="norm_cast",
    )(x, g.reshape(1, d))


def _resid_body(x_ref, f_ref, g_ref, gn_ref, xo_ref, no_ref, *, norm_next):
    xn = x_ref[...] + _rms(f_ref[...], g_ref[...])
    xo_ref[...] = xn
    if norm_next:
        no_ref[...] = _rms(xn, gn_ref[...]).astype(no_ref.dtype)
    else:
        no_ref[...] = xn.astype(no_ref.dtype)


def _resid_norm(x, f, g, g_next, *, tm=256):
    m, d = x.shape
    tm = _tile(m, tm)
    norm_next = g_next is not None
    gn = g_next if norm_next else g
    row = pl.BlockSpec((tm, d), lambda i: (i, 0))
    vec = pl.BlockSpec((1, d), lambda i: (0, 0))
    return pl.pallas_call(
        functools.partial(_resid_body, norm_next=norm_next),
        grid=(m // tm,),
        in_specs=[row, row, vec, vec],
        out_specs=[row, row],
        out_shape=[jax.ShapeDtypeStruct((m, d), F32), jax.ShapeDtypeStruct((m, d), BF16)],
        compiler_params=_cparams(("parallel",)),
        name="resid_norm",
    )(x, f, g.reshape(1, d), gn.reshape(1, d))


def _head_allsum(p, heads):
    q = p + pltpu.roll(p, 2 * heads, axis=1)
    return q + pltpu.roll(q, heads, axis=1)


def _col_slices(x):
    return [x[:, i * LANES:(i + 1) * LANES] for i in range(x.shape[1] // LANES)]


def _prep_body(fm_ref, fl_ref, pm_ref, plo_ref, mum_ref, mul_ref, w0_ref, w2_ref, a0_ref, a2_ref,
               g2_ref, kk_ref, ka_ref,
               e_nkk, e_d, e_b, e_k, e_r, r_out, kf_out, v_out, g_out, *, seq_shift, heads):
    width = heads * RWKV_HEAD
    groups = LANES // heads

    def shifted(x, p_ref):
        if not seq_shift:
            return p_ref[...]
        prev = pltpu.roll(x, 1, axis=0)
        row = lax.broadcasted_iota(jnp.int32, x.shape, 0)
        return jnp.where(row == 0, p_ref[0], prev)

    x = fm_ref[...]
    xm = x + (shifted(x, pm_ref) - x) * mum_ref[...]
    lo = fl_ref[...]
    lm = lo + (shifted(lo, plo_ref) - lo) * mul_ref[...]

    r = xm[:, :width]
    k = xm[:, width:2 * width]
    v = xm[:, 2 * width:]

    u = w0_ref[...] + jnp.dot(jnp.tanh(lm[:, :DECAY_LORA]).astype(BF16), w2_ref[...],
                              preferred_element_type=F32)
    w_log = -(jnp.maximum(-u, 0.0) + jnp.log1p(jnp.exp(-jnp.abs(u)))) - 0.5
    decay = jnp.exp(-jnp.exp(w_log))
    a = jax.nn.sigmoid(a0_ref[...] + jnp.dot(lm[:, DECAY_LORA:2 * DECAY_LORA].astype(BF16),
                                             a2_ref[...], preferred_element_type=F32))
    g = jnp.dot(jax.nn.sigmoid(lm[:, DECAY_LORA:]).astype(BF16), g2_ref[...],
                preferred_element_type=F32)

    kk = k * kk_ref[...]
    sq = _col_slices(kk * kk)
    tot = sq[0]
    for s in sq[1:]:
        tot = tot + s
    den = jnp.maximum(jnp.sqrt(_head_allsum(tot, heads)), L2_EPS)
    kk = jnp.concatenate([c / den for c in _col_slices(kk)], axis=1)
    kf = k * (1.0 + (a - 1.0) * ka_ref[...])
    b = kk * a

    r_out[...] = r
    kf_out[...] = kf
    v_out[...] = v
    g_out[...] = g

    lane_group = lax.broadcasted_iota(jnp.int32, (x.shape[0], LANES), 1) // heads

    def expand(val, ref):
        for i, c in enumerate(_col_slices(val)):
            for gi in range(groups):
                ref[i * groups + gi] = _head_allsum(jnp.where(lane_group == gi, c, 0.0), heads)

    expand(-kk, e_nkk)
    expand(decay, e_d)
    expand(b, e_b)
    expand(kf, e_k)
    expand(r, e_r)


def _rwkv_prep(feats, lora, prev_m, prev_l, prm, *, seq_shift, heads, tm):
    m = feats.shape[0]
    width = heads * RWKV_HEAD
    tm = _tile(m, tm)
    nt = m // tm
    if seq_shift:
        pm_spec = pl.BlockSpec((1, 1, 3 * width), lambda i: (i, 0, 0))
        pl_spec = pl.BlockSpec((1, 1, LORA_PAD), lambda i: (i, 0, 0))
    else:
        pm_spec = pl.BlockSpec((tm, 3 * width), lambda i: (i, 0))
        pl_spec = pl.BlockSpec((tm, LORA_PAD), lambda i: (i, 0))

    def vec(n):
        return pl.BlockSpec((1, n), lambda i: (0, 0))

    def full(r_, c_):
        return pl.BlockSpec((r_, c_), lambda i: (0, 0))

    e_spec = pl.BlockSpec((RWKV_HEAD, tm, LANES), lambda i: (0, i, 0))
    c_spec = pl.BlockSpec((tm, width), lambda i: (i, 0))
    e_shape = jax.ShapeDtypeStruct((RWKV_HEAD, m, LANES), F32)
    c_shape = jax.ShapeDtypeStruct((m, width), F32)
    return pl.pallas_call(
        functools.partial(_prep_body, seq_shift=seq_shift, heads=heads),
        grid=(nt,),
        in_specs=[pl.BlockSpec((tm, 3 * width), lambda i: (i, 0)),
                  pl.BlockSpec((tm, LORA_PAD), lambda i: (i, 0)),
                  pm_spec, pl_spec, vec(3 * width), vec(LORA_PAD),
                  vec(width), full(DECAY_LORA, width), vec(width), full(DECAY_LORA, width),
                  full(LORA_PAD - DECAY_LORA, width), vec(width), vec(width)],
        out_specs=[e_spec] * 5 + [c_spec] * 4,
        out_shape=[e_shape] * 5 + [c_shape] * 4,
        compiler_params=_cparams(("parallel",)),
        name="rwkv_prep",
    )(feats, lora, prev_m, prev_l, prm["mu_m"], prm["mu_l"], prm["w0"], prm["w2"], prm["a0"],
      prm["a2"], prm["g2"], prm["k_k"], prm["k_a"])


N_ACC = 4


def _tree_sum(parts):
    while len(parts) > 1:
        parts = [parts[i] + parts[i + 1] for i in range(0, len(parts), 2)]
    return parts[0]


def _wkv_step(s_ref, nkk_ref, d_ref, b_ref, k_ref, r_ref, v_t, t):
    acc = [None] * N_ACC
    for kx in range(RWKV_HEAD):
        term = s_ref[kx] * nkk_ref[kx, pl.ds(t, 1), :]
        acc[kx % N_ACC] = term if acc[kx % N_ACC] is None else acc[kx % N_ACC] + term
    sa = _tree_sum(acc)
    yacc = [None] * N_ACC
    for kx in range(RWKV_HEAD):
        s_new = (s_ref[kx] * d_ref[kx, pl.ds(t, 1), :] + sa * b_ref[kx, pl.ds(t, 1), :]
                 + v_t * k_ref[kx, pl.ds(t, 1), :])
        s_ref[kx] = s_new
        term = s_new * r_ref[kx, pl.ds(t, 1), :]
        yacc[kx % N_ACC] = term if yacc[kx % N_ACC] is None else yacc[kx % N_ACC] + term
    return _tree_sum(yacc)


def _scan_seq_body(s0_ref, nkk_ref, d_ref, b_ref, k_ref, r_ref, v_ref, y_ref, sout_ref, s_ref, *,
                   tc):
    c = pl.program_id(0)

    @pl.when(c == 0)
    def _():
        s_ref[...] = s0_ref[...]

    def step(t, carry):
        y_ref[t] = _wkv_step(s_ref, nkk_ref, d_ref, b_ref, k_ref, r_ref, v_ref[t], t)
        return carry

    lax.fori_loop(0, tc, step, 0)

    @pl.when(c == pl.num_programs(0) - 1)
    def _():
        sout_ref[...] = s_ref[...]


def _wkv_scan_seq(s0, e_list, v, *, tc=64):
    t_len = v.shape[0]
    tc = _tile(t_len, tc)
    s_spec = pl.BlockSpec(s0.shape, lambda c: (0, 0, 0))
    e_spec = pl.BlockSpec((RWKV_HEAD, tc, LANES), lambda c: (0, c, 0))
    v_spec = pl.BlockSpec((tc,) + v.shape[1:], lambda c: (c, 0, 0))
    return pl.pallas_call(
        functools.partial(_scan_seq_body, tc=tc),
        grid=(t_len // tc,),
        in_specs=[s_spec] + [e_spec] * 5 + [v_spec],
        out_specs=[v_spec, s_spec],
        out_shape=[jax.ShapeDtypeStruct(v.shape, F32), jax.ShapeDtypeStruct(s0.shape, F32)],
        scratch_shapes=[pltpu.VMEM(s0.shape, F32)],
        compiler_params=_cparams(("arbitrary",)),
        name="wkv_scan",
    )(s0, *e_list, v)


def _scan_batch_body(s0_ref, nkk_ref, d_ref, b_ref, k_ref, r_ref, v_ref, y_ref, sout_ref):
    bidx = pl.program_id(0)
    s = sout_ref.at[0]
    s[...] = s0_ref[0]
    y_ref[0] = _wkv_step(s, nkk_ref, d_ref, b_ref, k_ref, r_ref, v_ref[0], bidx)


def _wkv_scan_batch(s0, e_list, v):
    nb = v.shape[0]
    s_spec = pl.BlockSpec((1,) + s0.shape[1:], lambda b: (b, 0, 0, 0))
    e_spec = pl.BlockSpec((RWKV_HEAD, nb, LANES), lambda b: (0, 0, 0))
    v_spec = pl.BlockSpec((1,) + v.shape[1:], lambda b: (b, 0, 0))
    return pl.pallas_call(
        _scan_batch_body,
        grid=(nb,),
        in_specs=[s_spec] + [e_spec] * 5 + [v_spec],
        out_specs=[v_spec, s_spec],
        out_shape=[jax.ShapeDtypeStruct(v.shape, F32), jax.ShapeDtypeStruct(s0.shape, F32)],
        compiler_params=_cparams(("arbitrary",)),
        name="wkv_step",
    )(s0, *e_list, v)


def _post_body(y_ref, r_ref, k_ref, v_ref, g_ref, lnw_ref, lnb_ref, rk_ref, o_ref, *, heads):
    inv_n = 1.0 / RWKV_HEAD

    def head_sum(x):
        cols = _col_slices(x)
        tot = cols[0]
        for c in cols[1:]:
            tot = tot + c
        return _head_allsum(tot, heads)

    def tiled(stat, like):
        return jnp.concatenate([stat] * (like.shape[1] // LANES), axis=1)

    y = y_ref[...]
    yc = y - tiled(head_sum(y) * inv_n, y)
    var = head_sum(yc * yc) * inv_n
    yn = yc * tiled(lax.rsqrt(var + GN_EPS), y) * lnw_ref[...] + lnb_ref[...]
    bonus = tiled(head_sum(r_ref[...] * k_ref[...] * rk_ref[...]), y) * v_ref[...]
    o_ref[...] = ((yn + bonus) * g_ref[...]).astype(o_ref.dtype)


def _rwkv_post(y, r, kf, v, g, prm, *, heads, tm=256):
    m, width = y.shape
    tm = _tile(m, tm)
    row = pl.BlockSpec((tm, width), lambda i: (i, 0))
    vec = pl.BlockSpec((1, width), lambda i: (0, 0))
    return pl.pallas_call(
        functools.partial(_post_body, heads=heads),
        grid=(m // tm,),
        in_specs=[row] * 5 + [vec] * 3,
        out_specs=row,
        out_shape=jax.ShapeDtypeStruct((m, width), BF16),
        compiler_params=_cparams(("parallel",)),
        name="rwkv_post",
    )(y, r, kf, v, g, prm["ln_w"], prm["ln_b"], prm["r_k"])


SB_TQ = 256
SB_SUB = 256
SB_TK = 1024


def _log_sigmoid_pair(z):
    lsz = jnp.minimum(z, 0.0) - jnp.log1p(jnp.exp(-jnp.abs(z)))
    return lsz, lsz - z


def _suffix_sum(ls, tri_ref):
    hi = ls.astype(BF16)
    lo = (ls - hi.astype(F32)).astype(BF16)
    tri = tri_ref[...]
    return (jnp.dot(hi, tri, preferred_element_type=F32)
            + jnp.dot(lo, tri, preferred_element_type=F32))


def _sbp_body(qi_ref, kb_ref, first_ref, last_ref, bias_ref, q_ref, k_ref, v_ref, tri_ref, g_ref,
              o_ref, qs_ref, acc_ref, carry_ref):
    h = pl.program_id(0)
    s = pl.program_id(1)
    qi = qi_ref[s]
    kb = kb_ref[s]
    tq, tk = q_ref.shape[0], k_ref.shape[0]

    @pl.when(first_ref[s] == 1)
    def _():
        qs_ref[...] = (q_ref[...] * (SB_HEAD ** -0.5)).astype(BF16)
        acc_ref[...] = jnp.zeros_like(acc_ref)
        carry_ref[...] = jnp.zeros_like(carry_ref)

    kmat = k_ref[...].astype(BF16)
    vmat = v_ref[...].astype(BF16)
    z = lax.dot_general(qs_ref[...], kmat, (((1,), (1,)), ((), ())),
                        preferred_element_type=F32) + bias_ref[h]
    qpos = qi * tq + lax.broadcasted_iota(jnp.int32, (tq, tk), 0)
    kpos = kb * tk + lax.broadcasted_iota(jnp.int32, (tq, tk), 1)
    causal = kpos < qpos
    lsz, lsn = _log_sigmoid_pair(z)
    ls = jnp.where(causal, lsn, 0.0)

    carry = carry_ref[...]
    n_sub = tri_ref.shape[0]
    w_parts = [None] * (tk // n_sub)
    for sub in reversed(range(tk // n_sub)):
        sl = slice(sub * n_sub, (sub + 1) * n_sub)
        ls_s = ls[:, sl]
        logw = lsz[:, sl] + _suffix_sum(ls_s, tri_ref) + carry
        w_parts[sub] = jnp.where(causal[:, sl], jnp.exp(logw), 0.0).astype(BF16)
        carry = carry + jnp.sum(ls_s, axis=1, keepdims=True)
    carry_ref[...] = carry
    acc_ref[...] += jnp.dot(jnp.concatenate(w_parts, axis=1), vmat, preferred_element_type=F32)

    @pl.when(last_ref[s] == 1)
    def _():
        o_ref[...] = _rms(acc_ref[...], g_ref[0]).astype(o_ref.dtype)


def _sb_schedule(t_len, tq, tk):
    qi, kb, first, last = [], [], [], []
    for i in range(t_len // tq):
        top = (i * tq + tq - 1) // tk
        for j in range(top, -1, -1):
            qi.append(i)
            kb.append(j)
            first.append(int(j == top))
            last.append(int(j == 0))
    return [jnp.asarray(np.asarray(a, np.int32)) for a in (qi, kb, first, last)]


def _sb_prompt(qkv, q_col, bias, gain, *, heads):
    t_len = qkv.shape[0]
    tq = _tile(t_len, SB_TQ)
    tk = _tile(t_len, SB_TK)
    sched = _sb_schedule(t_len, tq, tk)
    nsteps = sched[0].shape[0]
    sub = min(SB_SUB, tk)
    tri = (lax.broadcasted_iota(jnp.int32, (sub, sub), 0)
           > lax.broadcasted_iota(jnp.int32, (sub, sub), 1)).astype(BF16)
    grid_spec = pltpu.PrefetchScalarGridSpec(
        num_scalar_prefetch=5,
        grid=(heads, nsteps),
        in_specs=[
            pl.BlockSpec((tq, SB_HEAD), lambda h, s, qi, kb, f, l, b: (qi[s], q_col + h)),
            pl.BlockSpec((tk, SB_HEAD), lambda h, s, qi, kb, f, l, b: (kb[s], q_col + heads + h)),
            pl.BlockSpec((tk, SB_HEAD), lambda h, s, qi, kb, f, l, b: (kb[s], q_col + 2 * heads + h)),
            pl.BlockSpec((sub, sub), lambda h, s, qi, kb, f, l, b: (0, 0)),
            pl.BlockSpec((1, 1, SB_HEAD), lambda h, s, qi, kb, f, l, b: (h, 0, 0)),
        ],
        out_specs=pl.BlockSpec((tq, SB_HEAD), lambda h, s, qi, kb, f, l, b: (qi[s], h)),
        scratch_shapes=[pltpu.VMEM((tq, SB_HEAD), BF16), pltpu.VMEM((tq, SB_HEAD), F32),
                        pltpu.VMEM((tq, 1), F32)],
    )
    return pl.pallas_call(
        _sbp_body,
        grid_spec=grid_spec,
        out_shape=jax.ShapeDtypeStruct((t_len, heads * SB_HEAD), BF16),
        compiler_params=_cparams(("parallel", "arbitrary")),
        name="sb_prompt",
    )(*sched, bias, qkv, qkv, qkv, tri, gain.reshape(heads, 1, SB_HEAD))


def _sbs_body(pt_ref, q_ref, k_ref, v_ref, tri_ref, bias_ref, g_ref, o_ref, wq_ref, acc_ref,
              carry_ref, *, heads):
    p = pl.program_id(1)
    width = heads * SB_HEAD
    head_mask = (lax.broadcasted_iota(jnp.int32, (heads, width), 1) // SB_HEAD
                 == lax.broadcasted_iota(jnp.int32, (heads, width), 0))

    @pl.when(p == 0)
    def _():
        qrow = q_ref[0] * (SB_HEAD ** -0.5)
        wq_ref[...] = jnp.where(head_mask, qrow, 0.0).astype(BF16)
        acc_ref[...] = jnp.zeros_like(acc_ref)
        carry_ref[...] = jnp.zeros_like(carry_ref)

    kmat = k_ref[0].astype(BF16)
    vmat = v_ref[0].astype(BF16)
    z = lax.dot_general(wq_ref[...], kmat, (((1,), (1,)), ((), ())),
                        preferred_element_type=F32) + bias_ref[...]
    lsz, lsn = _log_sigmoid_pair(z)
    carry = carry_ref[...]
    w = jnp.exp(lsz + _suffix_sum(lsn, tri_ref) + carry).astype(BF16)
    carry_ref[...] = carry + jnp.sum(lsn, axis=1, keepdims=True)
    acc_ref[...] += jnp.dot(w, vmat, preferred_element_type=F32)

    @pl.when(p == pl.num_programs(1) - 1)
    def _():
        sel = jnp.where(head_mask, acc_ref[...], 0.0)
        ms = jnp.sum(sel * sel, axis=1, keepdims=True) * (1.0 / SB_HEAD)
        normed = sel * lax.rsqrt(ms + RMS_EPS)
        o_ref[0] = (jnp.sum(normed, axis=0, keepdims=True) * g_ref[...]).astype(o_ref.dtype)


def _sb_sample(q, cache_k, cache_v, page_table, bias, gain, *, heads):
    nb, width = q.shape
    n_pages = page_table.shape[1]
    page = cache_k.shape[1]
    tri = (lax.broadcasted_iota(jnp.int32, (page, page), 0)
           > lax.broadcasted_iota(jnp.int32, (page, page), 1)).astype(BF16)
    kv_spec = pl.BlockSpec((1, page, width), lambda b, p, pt: (pt[b, n_pages - 1 - p], 0, 0))
    grid_spec = pltpu.PrefetchScalarGridSpec(
        num_scalar_prefetch=1,
        grid=(nb, n_pages),
        in_specs=[pl.BlockSpec((1, 1, width), lambda b, p, pt: (b, 0, 0)),
                  kv_spec, kv_spec,
                  pl.BlockSpec((page, page), lambda b, p, pt: (0, 0)),
                  pl.BlockSpec((heads, 1), lambda b, p, pt: (0, 0)),
                  pl.BlockSpec((1, width), lambda b, p, pt: (0, 0))],
        out_specs=pl.BlockSpec((1, 1, width), lambda b, p, pt: (b, 0, 0)),
        scratch_shapes=[pltpu.VMEM((heads, width), BF16), pltpu.VMEM((heads, width), F32),
                        pltpu.VMEM((heads, 1), F32)],
    )
    out = pl.pallas_call(
        functools.partial(_sbs_body, heads=heads),
        grid_spec=grid_spec,
        out_shape=jax.ShapeDtypeStruct((nb, 1, width), BF16),
        compiler_params=_cparams(("parallel", "arbitrary")),
        name="sb_sample",
    )(page_table, q.reshape(nb, 1, width), cache_k, cache_v, tri, bias.reshape(heads, 1),
      gain.reshape(1, width))
    return out.reshape(nb, width)


def _perm_cols(w, heads):
    lead = w.shape[:-1]
    return w.reshape(lead + (heads, RWKV_HEAD)).swapaxes(-1, -2).reshape(lead + (heads * RWKV_HEAD,))


def _unperm_cols(w, heads):
    lead = w.shape[:-1]
    return w.reshape(lead + (RWKV_HEAD, heads)).swapaxes(-1, -2).reshape(lead + (heads * RWKV_HEAD,))


def _state_in(s):
    heads = s.shape[-3]
    lead = s.shape[:-3]
    s = jnp.moveaxis(s, -3, -1)
    s = jnp.swapaxes(s, -3, -2)
    return s.reshape(lead + (RWKV_HEAD, RWKV_HEAD * heads // LANES, LANES))


def _state_out(s, heads):
    lead = s.shape[:-3]
    s = s.reshape(lead + (RWKV_HEAD, RWKV_HEAD, heads))
    s = jnp.swapaxes(s, -3, -2)
    return jnp.moveaxis(s, -1, -3)


def _layer(x, pe, shift_prev, wkv0, attend, wts, *, seq_shift):
    heads = wts["heads"]
    width = heads * RWKV_HEAD
    sb_heads = wts["sb_heads"]
    sb_width = sb_heads * SB_HEAD
    m = x.shape[0]

    h = _norm_cast(x, wts["norm_mix_pre"])
    proj = _matmul(h, wts["w_main"], name="mm_in")
    lora = _matmul(h, wts["w_lora"], name="mm_lora")

    feats_last = jnp.concatenate([_unperm_cols(proj[-1, :width], heads),
                                  _unperm_cols(proj[-1, width:2 * width], heads),
                                  _unperm_cols(proj[-1, 2 * width:3 * width], heads),
                                  lora[-1, :DECAY_LORA + AAA_LORA + GATE_LORA]])
    if seq_shift:
        tm = _tile(m, 64)
        sp = shift_prev.reshape(1, -1)
        prev_m = jnp.concatenate([sp[:, :3 * width], proj[tm - 1:m - 1:tm, :3 * width]], axis=0)
        prev_l = jnp.concatenate([sp[:, 3 * width:], lora[tm - 1:m - 1:tm]], axis=0)
        prev_m, prev_l = prev_m[:, None, :], prev_l[:, None, :]
        shift_new = feats_last[None]
    else:
        tm = m
        prev_m, prev_l = shift_prev[:, :3 * width], shift_prev[:, 3 * width:]
        shift_new = jnp.concatenate(
            [_unperm_cols(proj[:, :width], heads), _unperm_cols(proj[:, width:2 * width], heads),
             _unperm_cols(proj[:, 2 * width:3 * width], heads),
             lora[:, :DECAY_LORA + AAA_LORA + GATE_LORA]], axis=1)

    outs = _rwkv_prep(proj, lora, prev_m, prev_l, wts, seq_shift=seq_shift, heads=heads, tm=tm)
    e_list, (r_c, kf_c, v_c, g_c) = outs[:5], outs[5:]
    v_rows = v_c.reshape(m, width // LANES, LANES)
    if seq_shift:
        y, s_fin = _wkv_scan_seq(_state_in(wkv0[0]), e_list, v_rows)
        wkv_new = _state_out(s_fin, heads)[None]
    else:
        y, s_fin = _wkv_scan_batch(_state_in(wkv0), e_list, v_rows)
        wkv_new = _state_out(s_fin, heads)
    r_out = _rwkv_post(y.reshape(m, width), r_c, kf_c, v_c, g_c, wts, heads=heads)

    q_col = 3 * width // LANES
    a_out = attend(proj, q_col)
    kh = proj[:, 3 * width + sb_width:3 * width + 2 * sb_width].reshape(m, sb_heads, SB_HEAD)
    vh = proj[:, 3 * width + 2 * sb_width:].reshape(m, sb_heads, SB_HEAD)

    mix = _matmul_cat(r_out, a_out, wts["w_out"])
    x1, h2 = _resid_norm(x, mix, wts["norm_mix_post"], wts["norm_ffn_pre"])
    up = _matmul(h2, wts["w_up"], out_dtype=BF16, relu2=True, name="mm_up")
    f = _matmul(up, wts["w_down"], name="mm_down")
    x2, x2b = _resid_norm(x1, f, wts["norm_ffn_post"], None)
    out = _ple(x2b, wts["w_ple_gate"], x2, pe.astype(BF16), wts["w_ple_proj"])
    return out, shift_new, wkv_new, kh, vh


def _shift_in(s, width, heads):
    pad = LORA_PAD - (s.shape[1] - 3 * width)
    return jnp.concatenate([_perm_cols(s[:, :width], heads),
                            _perm_cols(s[:, width:2 * width], heads),
                            _perm_cols(s[:, 2 * width:3 * width], heads),
                            s[:, 3 * width:], jnp.zeros((s.shape[0], pad), s.dtype)], axis=1)


def kernel(x_prompt, x_sample, p_prompt, p_sample, state_rwkv_shift, state_rwkv_wkv, cache_k, cache_v, page_table, norm_mix_pre, norm_mix_post, norm_ffn_pre, norm_ffn_post, w_in, rwkv_mu, rwkv_w0, rwkv_w2, rwkv_a0, rwkv_a2, rwkv_g2, rwkv_k_k, rwkv_k_a, rwkv_r_k, rwkv_ln_w, rwkv_ln_b, sb_norm, sb_bias, w_out, w_up, w_down, w_ple_gate, w_ple_proj):
    depth = w_in.shape[0]
    heads = rwkv_r_k.shape[1]
    width = heads * RWKV_HEAD
    sb_heads = sb_bias.shape[1]
    sb_width = sb_heads * SB_HEAD
    nb, t_len, d_model = x_prompt.shape
    db = x_sample.shape[0]
    assert nb == 1 and x_sample.shape[1] == 1 and LANES == 4 * heads
    rwkv_proj = rwkv_mu.shape[1]
    n_lora = rwkv_proj - 3 * width
    assert n_lora == DECAY_LORA + AAA_LORA + GATE_LORA and sb_width + width == w_out.shape[1]

    yp, ys = x_prompt[0], x_sample[:, 0]
    res = [[] for _ in range(8)]
    for i in range(depth):
        wi = w_in[i]
        lpad = jnp.zeros((d_model, LORA_PAD - n_lora), wi.dtype)
        mu = rwkv_mu[i]
        a2p = jnp.concatenate([rwkv_a2[i], jnp.zeros((DECAY_LORA - AAA_LORA, width), F32)], axis=0)
        g2p = jnp.concatenate([jnp.zeros((AAA_LORA, width), F32), rwkv_g2[i],
                               jnp.zeros((LORA_PAD - n_lora, width), F32)], axis=0)
        wo = w_out[i]
        wts = dict(
            heads=heads, sb_heads=sb_heads,
            norm_mix_pre=norm_mix_pre[i], norm_mix_post=norm_mix_post[i],
            norm_ffn_pre=norm_ffn_pre[i], norm_ffn_post=norm_ffn_post[i],
            w_main=jnp.concatenate([_perm_cols(wi[:, :width], heads),
                                    _perm_cols(wi[:, width:2 * width], heads),
                                    _perm_cols(wi[:, 2 * width:3 * width], heads),
                                    wi[:, rwkv_proj:]], axis=1).astype(BF16),
            w_lora=jnp.concatenate([wi[:, 3 * width:rwkv_proj], lpad], axis=1).astype(BF16),
            mu_m=jnp.concatenate([_perm_cols(mu[:width], heads), _perm_cols(mu[width:2 * width], heads),
                                  _perm_cols(mu[2 * width:3 * width], heads)]).reshape(1, -1),
            mu_l=jnp.concatenate([mu[3 * width:], jnp.zeros((LORA_PAD - n_lora,), F32)]).reshape(1, -1),
            w0=_perm_cols(rwkv_w0[i], heads).reshape(1, -1),
            w2=_perm_cols(rwkv_w2[i], heads).astype(BF16),
            a0=_perm_cols(rwkv_a0[i], heads).reshape(1, -1),
            a2=_perm_cols(a2p, heads).astype(BF16),
            g2=_perm_cols(g2p, heads).astype(BF16),
            k_k=_perm_cols(rwkv_k_k[i], heads).reshape(1, -1),
            k_a=_perm_cols(rwkv_k_a[i], heads).reshape(1, -1),
            r_k=_perm_cols(rwkv_r_k[i].reshape(-1), heads).reshape(1, -1),
            ln_w=_perm_cols(rwkv_ln_w[i], heads).reshape(1, -1),
            ln_b=_perm_cols(rwkv_ln_b[i], heads).reshape(1, -1),
            w_out=jnp.concatenate([wo[:width].reshape(heads, RWKV_HEAD, -1).swapaxes(0, 1)
                                   .reshape(width, -1), wo[width:]], axis=0).astype(BF16),
            w_up=w_up[i].astype(BF16), w_down=w_down[i].astype(BF16),
            w_ple_gate=w_ple_gate[i].astype(BF16), w_ple_proj=w_ple_proj[i].astype(BF16),
        )
        bias, gain = sb_bias[i], sb_norm[i]

        attend_p = lambda proj, q_col: _sb_prompt(proj, q_col, bias, gain, heads=sb_heads)
        zero_shift = _shift_in(jnp.zeros((1, rwkv_proj), F32), width, heads)
        zero_wkv = jnp.zeros((1, heads, RWKV_HEAD, RWKV_HEAD), F32)
        yp, sp, wp, kp, vp = _layer(yp, p_prompt[i, 0], zero_shift, zero_wkv, attend_p, wts,
                                    seq_shift=True)

        ck = cache_k[i].reshape(cache_k.shape[1], cache_k.shape[2], sb_width)
        cv = cache_v[i].reshape(cache_v.shape[1], cache_v.shape[2], sb_width)
        attend_s = lambda proj, q_col: _sb_sample(
            proj[:, q_col * LANES:q_col * LANES + sb_width], ck, cv, page_table, bias,
            gain.reshape(-1), heads=sb_heads)
        ys, ss, ws, kn, vn = _layer(ys, p_sample[i, :, 0], _shift_in(state_rwkv_shift[i], width, heads),
                                    state_rwkv_wkv[i], attend_s, wts, seq_shift=False)
        for lst, val in zip(res, (sp, ss, wp, ws, kp[None], vp[None], kn[:, None], vn[:, None])):
            lst.append(val)
    return (yp[None], ys[:, None]) + tuple(jnp.stack(r) for r in res)
```

```python
import functools

import jax
import jax.numpy as jnp
import numpy as np
from jax import lax
from jax.experimental import pallas as pl
from jax.experimental.pallas import tpu as pltpu

F32 = jnp.float32
BF16 = jnp.bfloat16

RMS_EPS = 1e-6
GN_EPS = 64e-5
L2_EPS = 1e-12

LANES = 128
RWKV_HEAD = 64
SB_HEAD = 128
DECAY_LORA = 128
AAA_LORA = 96
GATE_LORA = 256
LORA_PAD = 512
VMEM_LIMIT = 52 * 1024 * 1024


def _cparams(sem):
    return pltpu.CompilerParams(dimension_semantics=sem, vmem_limit_bytes=VMEM_LIMIT)


def _tile(n, pref):
    if n <= pref:
        return n
    t = pref
    while n % t:
        t //= 2
    return t


def _mm_body(a_ref, b_ref, o_ref, acc_ref, *, nk, relu2):
    part = jnp.dot(a_ref[...], b_ref[...], preferred_element_type=F32)

    def finish(acc):
        if relu2:
            acc = jnp.square(jnp.maximum(acc, 0.0))
        o_ref[...] = acc.astype(o_ref.dtype)

    if nk == 1:
        finish(part)
    else:
        k = pl.program_id(2)

        @pl.when(k == 0)
        def _():
            acc_ref[...] = part

        @pl.when(k > 0)
        def _():
            acc_ref[...] += part

        @pl.when(k == nk - 1)
        def _():
            finish(acc_ref[...])


def _matmul(a, b, *, out_dtype=F32, relu2=False, tm=1024, tn=1024, tk=2048, name="mm"):
    m, kdim = a.shape
    n = b.shape[1]
    tm, tn = _tile(m, tm), _tile(n, tn)
    tk = kdim if kdim <= 4096 else _tile(kdim, tk)
    nk = kdim // tk
    return pl.pallas_call(
        functools.partial(_mm_body, nk=nk, relu2=relu2),
        grid=(n // tn, m // tm, nk),
        in_specs=[pl.BlockSpec((tm, tk), lambda j, i, k: (i, k)),
                  pl.BlockSpec((tk, tn), lambda j, i, k: (k, j))],
        out_specs=pl.BlockSpec((tm, tn), lambda j, i, k: (i, j)),
        out_shape=jax.ShapeDtypeStruct((m, n), out_dtype),
        scratch_shapes=[pltpu.VMEM((tm, tn) if nk > 1 else (8, LANES), F32)],
        compiler_params=_cparams(("parallel", "parallel", "arbitrary")),
        name=name,
    )(a, b)


def _mm2_body(a1_ref, a2_ref, b1_ref, b2_ref, o_ref):
    o_ref[...] = (jnp.dot(a1_ref[...], b1_ref[...], preferred_element_type=F32)
                  + jnp.dot(a2_ref[...], b2_ref[...], preferred_element_type=F32))


def _matmul_cat(a1, a2, b, *, tm=1024, tn=1024):
    m, k1 = a1.shape
    k2 = a2.shape[1]
    n = b.shape[1]
    tm, tn = _tile(m, tm), _tile(n, tn)
    nb1 = k1 // k2
    assert k1 == nb1 * k2
    return pl.pallas_call(
        _mm2_body,
        grid=(n // tn, m // tm),
        in_specs=[pl.BlockSpec((tm, k1), lambda j, i: (i, 0)),
                  pl.BlockSpec((tm, k2), lambda j, i: (i, 0)),
                  pl.BlockSpec((k1, tn), lambda j, i: (0, j)),
                  pl.BlockSpec((k2, tn), lambda j, i: (nb1, j))],
        out_specs=pl.BlockSpec((tm, tn), lambda j, i: (i, j)),
        out_shape=jax.ShapeDtypeStruct((m, n), F32),
        compiler_params=_cparams(("parallel", "parallel")),
        name="mm_out",
    )(a1, a2, b, b)


def _ple_body(a_ref, b_ref, x_ref, pe_ref, wp_ref, o_ref):
    gate = jnp.dot(a_ref[...], b_ref[...], preferred_element_type=F32)
    proj = jnp.dot(pe_ref[...], wp_ref[...], preferred_element_type=F32)
    o_ref[...] = x_ref[...] + jax.nn.sigmoid(gate) * proj


def _ple(xb, wg, x, pe, wp, *, tm=512, tn=1024):
    m, kdim = xb.shape
    n = wg.shape[1]
    kp = pe.shape[1]
    tm, tn = _tile(m, tm), _tile(n, tn)
    return pl.pallas_call(
        _ple_body,
        grid=(n // tn, m // tm),
        in_specs=[pl.BlockSpec((tm, kdim), lambda j, i: (i, 0)),
                  pl.BlockSpec((kdim, tn), lambda j, i: (0, j)),
                  pl.BlockSpec((tm, tn), lambda j, i: (i, j)),
                  pl.BlockSpec((tm, kp), lambda j, i: (i, 0)),
                  pl.BlockSpec((kp, tn), lambda j, i: (0, j))],
        out_specs=pl.BlockSpec((tm, tn), lambda j, i: (i, j)),
        out_shape=jax.ShapeDtypeStruct((m, n), F32),
        compiler_params=_cparams(("parallel", "parallel")),
        name="ple",
    )(xb, wg, x, pe, wp)


def _rms(x, g):
    return x * lax.rsqrt(jnp.mean(x * x, axis=-1, keepdims=True) + RMS_EPS) * g


def _norm_cast_body(x_ref, g_ref, o_ref):
    o_ref[...] = _rms(x_ref[...], g_ref[...]).astype(o_ref.dtype)


def _norm_cast(x, g, *, tm=256):
    m, d = x.shape
    tm = _tile(m, tm)
    return pl.pallas_call(
        _norm_cast_body,
        grid=(m // tm,),
        in_specs=[pl.BlockSpec((tm, d), lambda i: (i, 0)),
                  pl.BlockSpec((1, d), lambda i: (0, 0))],
        out_specs=pl.BlockSpec((tm, d), lambda i: (i, 0)),
        out_shape=jax.ShapeDtypeStruct((m, d), BF16),
        compiler_params=_cparams(("parallel",)),
        name="norm_cast",
    )(x, g.reshape(1, d))


def _resid_body(x_ref, f_ref, g_ref, gn_ref, xo_ref, no_ref, *, norm_next):
    xn = x_ref[...] + _rms(f_ref[...], g_ref[...])
    xo_ref[...] = xn
    if norm_next:
        no_ref[...] = _rms(xn, gn_ref[...]).astype(no_ref.dtype)
    else:
        no_ref[...] = xn.astype(no_ref.dtype)


def _resid_norm(x, f, g, g_next, *, tm=256):
    m, d = x.shape
    tm = _tile(m, tm)
    norm_next = g_next is not None
    gn = g_next if norm_next else g
    row = pl.BlockSpec((tm, d), lambda i: (i, 0))
    vec = pl.BlockSpec((1, d), lambda i: (0, 0))
    return pl.pallas_call(
        functools.partial(_resid_body, norm_next=norm_next),
        grid=(m // tm,),
        in_specs=[row, row, vec, vec],
        out_specs=[row, row],
        out_shape=[jax.ShapeDtypeStruct((m, d), F32), jax.ShapeDtypeStruct((m, d), BF16)],
        compiler_params=_cparams(("parallel",)),
        name="resid_norm",
    )(x, f, g.reshape(1, d), gn.reshape(1, d))


def _head_allsum(p, heads):
    q = p + pltpu.roll(p, 2 * heads, axis=1)
    return q + pltpu.roll(q, heads, axis=1)


def _col_slices(x):
    return [x[:, i * LANES:(i + 1) * LANES] for i in range(x.shape[1] // LANES)]


def _split3(x):
    hi = x.astype(BF16)
    r1 = x - hi.astype(F32)
    mid = r1.astype(BF16)
    lo = (r1 - mid.astype(F32)).astype(BF16)
    return hi, mid, lo


def _group_select(heads):
    src = np.arange(LANES)[:, None]
    dst = np.arange(2 * LANES)[None, :]
    mats = [(src == (2 * p + dst // LANES) * heads + dst % heads) for p in range(2)]
    return jnp.asarray(np.stack(mats), BF16)


def _prep_body(fm_ref, fl_ref, pm_ref, plo_ref, mum_ref, mul_ref, w0_ref, w2_ref, a0_ref, a2_ref,
               g2_ref, kk_ref, ka_ref, rk_ref, sel_ref,
               e_nkk, e_d, e_b, e_k, e_r, v_out, bonus_out, g_out, *, seq_shift, heads):
    width = heads * RWKV_HEAD

    def shifted(x, p_ref):
        if not seq_shift:
            return p_ref[...]
        prev = pltpu.roll(x, 1, axis=0)
        row = lax.broadcasted_iota(jnp.int32, x.shape, 0)
        return jnp.where(row == 0, p_ref[0], prev)

    x = fm_ref[...]
    xm = x + (shifted(x, pm_ref) - x) * mum_ref[...]
    lo = fl_ref[...]
    lm = lo + (shifted(lo, plo_ref) - lo) * mul_ref[...]

    r = xm[:, :width]
    k = xm[:, width:2 * width]
    v = xm[:, 2 * width:]

    u = w0_ref[...] + jnp.dot(jnp.tanh(lm[:, :DECAY_LORA]).astype(BF16), w2_ref[...],
                              preferred_element_type=F32)
    w_log = -(jnp.maximum(-u, 0.0) + jnp.log1p(jnp.exp(-jnp.abs(u)))) - 0.5
    decay = jnp.exp(-jnp.exp(w_log))
    a = jax.nn.sigmoid(a0_ref[...] + jnp.dot(lm[:, DECAY_LORA:2 * DECAY_LORA].astype(BF16),
                                             a2_ref[...], preferred_element_type=F32))
    g = jnp.dot(jax.nn.sigmoid(lm[:, DECAY_LORA:]).astype(BF16), g2_ref[...],
                preferred_element_type=F32)

    kk = k * kk_ref[...]
    sq = _col_slices(kk * kk)
    tot = sq[0]
    for s in sq[1:]:
        tot = tot + s
    den = jnp.maximum(jnp.sqrt(_head_allsum(tot, heads)), L2_EPS)
    kk = jnp.concatenate([c / den for c in _col_slices(kk)], axis=1)
    kf = k * (1.0 + (a - 1.0) * ka_ref[...])
    b = kk * a

    rk_parts = _col_slices(r * kf * rk_ref[...])
    rk_tot = rk_parts[0]
    for s in rk_parts[1:]:
        rk_tot = rk_tot + s
    rk_tot = _head_allsum(rk_tot, heads)
    v_out[...] = v
    bonus_out[...] = jnp.concatenate([rk_tot * c for c in _col_slices(v)], axis=1)
    g_out[...] = g

    rows = x.shape[0]

    def expand(val, ref):
        parts = _split3(jnp.concatenate(_col_slices(val), axis=0))
        for p in range(2):
            sel = sel_ref[p]
            out = (jnp.dot(parts[0], sel, preferred_element_type=F32)
                   + jnp.dot(parts[1], sel, preferred_element_type=F32)
                   + jnp.dot(parts[2], sel, preferred_element_type=F32))
            for i in range(width // LANES):
                for q in range(2):
                    ref[4 * i + 2 * p + q] = out[i * rows:(i + 1) * rows, q * LANES:(q + 1) * LANES]

    expand(-kk, e_nkk)
    expand(decay, e_d)
    expand(b, e_b)
    expand(kf, e_k)
    expand(r, e_r)


def _rwkv_prep(feats, lora, prev_m, prev_l, prm, *, seq_shift, heads, tm):
    m = feats.shape[0]
    width = heads * RWKV_HEAD
    tm = _tile(m, tm)
    nt = m // tm
    if seq_shift:
        pm_spec = pl.BlockSpec((1, 1, 3 * width), lambda i: (i, 0, 0))
        pl_spec = pl.BlockSpec((1, 1, LORA_PAD), lambda i: (i, 0, 0))
    else:
        pm_spec = pl.BlockSpec((tm, 3 * width), lambda i: (i, 0))
        pl_spec = pl.BlockSpec((tm, LORA_PAD), lambda i: (i, 0))

    def vec(n):
        return pl.BlockSpec((1, n), lambda i: (0, 0))

    def full(r_, c_):
        return pl.BlockSpec((r_, c_), lambda i: (0, 0))

    e_spec = pl.BlockSpec((RWKV_HEAD, tm, LANES), lambda i: (0, i, 0))
    c_spec = pl.BlockSpec((tm, width), lambda i: (i, 0))
    e_shape = jax.ShapeDtypeStruct((RWKV_HEAD, m, LANES), F32)
    c_shape = jax.ShapeDtypeStruct((m, width), F32)
    return pl.pallas_call(
        functools.partial(_prep_body, seq_shift=seq_shift, heads=heads),
        grid=(nt,),
        in_specs=[pl.BlockSpec((tm, 3 * width), lambda i: (i, 0)),
                  pl.BlockSpec((tm, LORA_PAD), lambda i: (i, 0)),
                  pm_spec, pl_spec, vec(3 * width), vec(LORA_PAD),
                  vec(width), full(DECAY_LORA, width), vec(width), full(DECAY_LORA, width),
                  full(LORA_PAD - DECAY_LORA, width), vec(width), vec(width), vec(width),
                  pl.BlockSpec((2, LANES, 2 * LANES), lambda i: (0, 0, 0))],
        out_specs=[e_spec] * 5 + [c_spec] * 3,
        out_shape=[e_shape] * 5 + [c_shape] * 3,
        compiler_params=_cparams(("parallel",)),
        name="rwkv_prep",
    )(feats, lora, prev_m, prev_l, prm["mu_m"], prm["mu_l"], prm["w0"], prm["w2"], prm["a0"],
      prm["a2"], prm["g2"], prm["k_k"], prm["k_a"], prm["r_k"], _group_select(heads))


N_ACC = 4


def _tree_sum(parts):
    while len(parts) > 1:
        parts = [parts[i] + parts[i + 1] for i in range(0, len(parts), 2)]
    return parts[0]


def _wkv_sa(s_ref, nkk_ref, t):
    acc = [None] * N_ACC
    for kx in range(RWKV_HEAD):
        term = s_ref[kx] * nkk_ref[kx, pl.ds(t, 1), :]
        acc[kx % N_ACC] = term if acc[kx % N_ACC] is None else acc[kx % N_ACC] + term
    return _tree_sum(acc)


def _wkv_step(s_ref, nkk_ref, d_ref, b_ref, k_ref, r_ref, v_t, sa, t, t_next):
    yacc = [None] * N_ACC
    sacc = [None] * N_ACC
    for kx in range(RWKV_HEAD):
        s_new = (s_ref[kx] * d_ref[kx, pl.ds(t, 1), :] + sa * b_ref[kx, pl.ds(t, 1), :]
                 + v_t * k_ref[kx, pl.ds(t, 1), :])
        s_ref[kx] = s_new
        term = s_new * r_ref[kx, pl.ds(t, 1), :]
        yacc[kx % N_ACC] = term if yacc[kx % N_ACC] is None else yacc[kx % N_ACC] + term
        if t_next is not None:
            term = s_new * nkk_ref[kx, pl.ds(t_next, 1), :]
            sacc[kx % N_ACC] = term if sacc[kx % N_ACC] is None else sacc[kx % N_ACC] + term
    return _tree_sum(yacc), (None if t_next is None else _tree_sum(sacc))


def _scan_seq_body(s0_ref, nkk_ref, d_ref, b_ref, k_ref, r_ref, v_ref, y_ref, sout_ref, s_ref, *,
                   tc):
    c = pl.program_id(0)

    @pl.when(c == 0)
    def _():
        s_ref[...] = s0_ref[...]

    def step(t, sa):
        y_ref[t], sa_next = _wkv_step(s_ref, nkk_ref, d_ref, b_ref, k_ref, r_ref, v_ref[t], sa, t,
                                      jnp.minimum(t + 1, tc - 1))
        return sa_next

    lax.fori_loop(0, tc, step, _wkv_sa(s_ref, nkk_ref, 0), unroll=4)

    @pl.when(c == pl.num_programs(0) - 1)
    def _():
        sout_ref[...] = s_ref[...]


def _wkv_scan_seq(s0, e_list, v, *, tc=64):
    t_len = v.shape[0]
    tc = _tile(t_len, tc)
    s_spec = pl.BlockSpec(s0.shape, lambda c: (0, 0, 0))
    e_spec = pl.BlockSpec((RWKV_HEAD, tc, LANES), lambda c: (0, c, 0))
    v_spec = pl.BlockSpec((tc,) + v.shape[1:], lambda c: (c, 0, 0))
    return pl.pallas_call(
        functools.partial(_scan_seq_body, tc=tc),
        grid=(t_len // tc,),
        in_specs=[s_spec] + [e_spec] * 5 + [v_spec],
        out_specs=[v_spec, s_spec],
        out_shape=[jax.ShapeDtypeStruct(v.shape, F32), jax.ShapeDtypeStruct(s0.shape, F32)],
        scratch_shapes=[pltpu.VMEM(s0.shape, F32)],
        compiler_params=_cparams(("arbitrary",)),
        name="wkv_scan",
    )(s0, *e_list, v)


def _scan_batch_body(s0_ref, nkk_ref, d_ref, b_ref, k_ref, r_ref, v_ref, y_ref, sout_ref):
    bidx = pl.program_id(0)
    s = sout_ref.at[0]
    s[...] = s0_ref[0]
    y_ref[0], _ = _wkv_step(s, nkk_ref, d_ref, b_ref, k_ref, r_ref, v_ref[0],
                            _wkv_sa(s, nkk_ref, bidx), bidx, None)


def _wkv_scan_batch(s0, e_list, v):
    nb = v.shape[0]
    s_spec = pl.BlockSpec((1,) + s0.shape[1:], lambda b: (b, 0, 0, 0))
    e_spec = pl.BlockSpec((RWKV_HEAD, nb, LANES), lambda b: (0, 0, 0))
    v_spec = pl.BlockSpec((1,) + v.shape[1:], lambda b: (b, 0, 0))
    return pl.pallas_call(
        _scan_batch_body,
        grid=(nb,),
        in_specs=[s_spec] + [e_spec] * 5 + [v_spec],
        out_specs=[v_spec, s_spec],
        out_shape=[jax.ShapeDtypeStruct(v.shape, F32), jax.ShapeDtypeStruct(s0.shape, F32)],
        compiler_params=_cparams(("arbitrary",)),
        name="wkv_step",
    )(s0, *e_list, v)


def _post_body(y_ref, bonus_ref, g_ref, lnw_ref, lnb_ref, o_ref, *, heads):
    inv_n = 1.0 / RWKV_HEAD

    def head_sum(x):
        cols = _col_slices(x)
        tot = cols[0]
        for c in cols[1:]:
            tot = tot + c
        return _head_allsum(tot, heads)

    def tiled(stat, like):
        return jnp.concatenate([stat] * (like.shape[1] // LANES), axis=1)

    y = y_ref[...]
    yc = y - tiled(head_sum(y) * inv_n, y)
    var = head_sum(yc * yc) * inv_n
    yn = yc * tiled(lax.rsqrt(var + GN_EPS), y) * lnw_ref[...] + lnb_ref[...]
    o_ref[...] = ((yn + bonus_ref[...]) * g_ref[...]).astype(o_ref.dtype)


def _rwkv_post(y, bonus, g, prm, *, heads, tm=256):
    m, width = y.shape
    tm = _tile(m, tm)
    row = pl.BlockSpec((tm, width), lambda i: (i, 0))
    vec = pl.BlockSpec((1, width), lambda i: (0, 0))
    return pl.pallas_call(
        functools.partial(_post_body, heads=heads),
        grid=(m // tm,),
        in_specs=[row] * 3 + [vec] * 2,
        out_specs=row,
        out_shape=jax.ShapeDtypeStruct((m, width), BF16),
        compiler_params=_cparams(("parallel",)),
        name="rwkv_post",
    )(y, bonus, g, prm["ln_w"], prm["ln_b"])


SB_TILE = 512
SB_SUB = 256
LOG2E = 1.4426950408889634


def _log_sigmoid_pair(z):
    lsz = jnp.minimum(z, 0.0) - jnp.log(1.0 + jnp.exp2(jnp.abs(z) * (-LOG2E)))
    return lsz, lsz - z


def _suffix_sum(ls, tri_ref):
    hi = ls.astype(BF16)
    lo = (ls - hi.astype(F32)).astype(BF16)
    tri = tri_ref[...]
    return (jnp.dot(hi, tri, preferred_element_type=F32)
            + jnp.dot(lo, tri, preferred_element_type=F32))


def _sbp_body(qi_ref, kb_ref, bias_ref, q_ref, k_ref, v_ref, tri_ref, g_ref,
              o_ref, qs_ref, acc_ref, carry_ref):
    h = pl.program_id(0)
    s = pl.program_id(1)
    tile = q_ref.shape[0]
    n_sub = tri_ref.shape[0]
    diagonal = kb_ref[s] == qi_ref[s]

    def visit(masked):
        kmat = k_ref[...].astype(BF16)
        vmat = v_ref[...].astype(BF16)
        z = lax.dot_general(qs_ref[...], kmat, (((1,), (1,)), ((), ())),
                            preferred_element_type=F32) + bias_ref[h]
        lsz, lsn = _log_sigmoid_pair(z)
        if masked:
            causal = (lax.broadcasted_iota(jnp.int32, (tile, tile), 1)
                      < lax.broadcasted_iota(jnp.int32, (tile, tile), 0))
            lsn = jnp.where(causal, lsn, 0.0)
        carry = carry_ref[...]
        w_parts = [None] * (tile // n_sub)
        for sub in reversed(range(tile // n_sub)):
            sl = slice(sub * n_sub, (sub + 1) * n_sub)
            w = jnp.exp(lsz[:, sl] + _suffix_sum(lsn[:, sl], tri_ref) + carry)
            if masked:
                w = jnp.where(causal[:, sl], w, 0.0)
            w_parts[sub] = w.astype(BF16)
            carry = carry + jnp.sum(lsn[:, sl], axis=1, keepdims=True)
        carry_ref[...] = carry
        acc_ref[...] += jnp.dot(jnp.concatenate(w_parts, axis=1), vmat,
                                preferred_element_type=F32)

    @pl.when(diagonal)
    def _():
        qs_ref[...] = (q_ref[...] * (SB_HEAD ** -0.5)).astype(BF16)
        acc_ref[...] = jnp.zeros_like(acc_ref)
        carry_ref[...] = jnp.zeros_like(carry_ref)
        visit(True)

    @pl.when(jnp.logical_not(diagonal))
    def _():
        visit(False)

    @pl.when(kb_ref[s] == 0)
    def _():
        o_ref[...] = _rms(acc_ref[...], g_ref[0]).astype(o_ref.dtype)


def _sb_schedule(n_tiles):
    qi = [i for i in range(n_tiles) for _ in range(i + 1)]
    kb = [j for i in range(n_tiles) for j in range(i, -1, -1)]
    return [jnp.asarray(np.asarray(a, np.int32)) for a in (qi, kb)]


def _sb_prompt(qkv, q_col, bias, gain, *, heads):
    t_len = qkv.shape[0]
    tile = _tile(t_len, SB_TILE)
    sched = _sb_schedule(t_len // tile)
    nsteps = sched[0].shape[0]
    sub = min(SB_SUB, tile)
    tri = (lax.broadcasted_iota(jnp.int32, (sub, sub), 0)
           > lax.broadcasted_iota(jnp.int32, (sub, sub), 1)).astype(BF16)
    grid_spec = pltpu.PrefetchScalarGridSpec(
        num_scalar_prefetch=3,
        grid=(heads, nsteps),
        in_specs=[
            pl.BlockSpec((tile, SB_HEAD), lambda h, s, qi, kb, b: (qi[s], q_col + h)),
            pl.BlockSpec((tile, SB_HEAD), lambda h, s, qi, kb, b: (kb[s], q_col + heads + h)),
            pl.BlockSpec((tile, SB_HEAD), lambda h, s, qi, kb, b: (kb[s], q_col + 2 * heads + h)),
            pl.BlockSpec((sub, sub), lambda h, s, qi, kb, b: (0, 0)),
            pl.BlockSpec((1, 1, SB_HEAD), lambda h, s, qi, kb, b: (h, 0, 0)),
        ],
        out_specs=pl.BlockSpec((tile, SB_HEAD), lambda h, s, qi, kb, b: (qi[s], h)),
        scratch_shapes=[pltpu.VMEM((tile, SB_HEAD), BF16), pltpu.VMEM((tile, SB_HEAD), F32),
                        pltpu.VMEM((tile, 1), F32)],
    )
    return pl.pallas_call(
        _sbp_body,
        grid_spec=grid_spec,
        out_shape=jax.ShapeDtypeStruct((t_len, heads * SB_HEAD), BF16),
        compiler_params=_cparams(("parallel", "arbitrary")),
        name="sb_prompt",
    )(*sched, bias, qkv, qkv, qkv, tri, gain.reshape(heads, 1, SB_HEAD))


SBS_PAGES = 4


def _sbs_body(pt_ref, q_ref, *refs, heads, pages):
    k_refs, v_refs = refs[:pages], refs[pages:2 * pages]
    tri_ref, bias_ref, g_ref, o_ref, qs_ref, acc_ref, carry_ref = refs[2 * pages:]
    p = pl.program_id(1)
    rows = k_refs[0].shape[1] * heads
    n_sub = tri_ref.shape[0]
    own_head = (lax.broadcasted_iota(jnp.int32, (heads, rows), 1) % heads
                == lax.broadcasted_iota(jnp.int32, (heads, rows), 0))

    @pl.when(p == 0)
    def _():
        qs_ref[...] = (q_ref[0] * (SB_HEAD ** -0.5)).astype(BF16)
        acc_ref[...] = jnp.zeros_like(acc_ref)
        carry_ref[...] = jnp.zeros_like(carry_ref)

    carry = carry_ref[...]
    acc = acc_ref[...]
    for k_ref, v_ref in zip(k_refs, v_refs):
        kmat = k_ref[0].reshape(rows, SB_HEAD).astype(BF16)
        vmat = v_ref[0].reshape(rows, SB_HEAD).astype(BF16)
        z = lax.dot_general(qs_ref[...], kmat, (((1,), (1,)), ((), ())),
                            preferred_element_type=F32) + bias_ref[...]
        lsz, lsn = _log_sigmoid_pair(z)
        lsn = jnp.where(own_head, lsn, 0.0)
        w_parts = [None] * (rows // n_sub)
        for sub in reversed(range(rows // n_sub)):
            sl = slice(sub * n_sub, (sub + 1) * n_sub)
            w = jnp.exp(lsz[:, sl] + _suffix_sum(lsn[:, sl], tri_ref) + carry)
            w_parts[sub] = jnp.where(own_head[:, sl], w, 0.0).astype(BF16)
            carry = carry + jnp.sum(lsn[:, sl], axis=1, keepdims=True)
        acc = acc + jnp.dot(jnp.concatenate(w_parts, axis=1), vmat, preferred_element_type=F32)
    carry_ref[...] = carry
    acc_ref[...] = acc

    @pl.when(p == pl.num_programs(1) - 1)
    def _():
        o_ref[0] = _rms(acc_ref[...], g_ref[...]).astype(o_ref.dtype)


def _sb_sample(q, cache_k, cache_v, page_table, bias, gain, *, heads):
    nb = q.shape[0]
    n_pages = page_table.shape[1]
    page = cache_k.shape[1]
    sub = min(SB_SUB, page * heads)
    tri = (lax.broadcasted_iota(jnp.int32, (sub, sub), 0)
           > lax.broadcasted_iota(jnp.int32, (sub, sub), 1)).astype(BF16)
    pages = SBS_PAGES
    while n_pages % pages:
        pages //= 2

    def kv_spec(u):
        return pl.BlockSpec((1, page, heads, SB_HEAD),
                            lambda b, p, pt: (pt[b, n_pages - 1 - (p * pages + u)], 0, 0, 0))

    kv_specs = [kv_spec(u) for u in range(pages)]
    grid_spec = pltpu.PrefetchScalarGridSpec(
        num_scalar_prefetch=1,
        grid=(nb, n_pages // pages),
        in_specs=[pl.BlockSpec((1, heads, SB_HEAD), lambda b, p, pt: (b, 0, 0))]
                 + kv_specs + kv_specs
                 + [pl.BlockSpec((sub, sub), lambda b, p, pt: (0, 0)),
                    pl.BlockSpec((heads, 1), lambda b, p, pt: (0, 0)),
                    pl.BlockSpec((heads, SB_HEAD), lambda b, p, pt: (0, 0))],
        out_specs=pl.BlockSpec((1, heads, SB_HEAD), lambda b, p, pt: (b, 0, 0)),
        scratch_shapes=[pltpu.VMEM((heads, SB_HEAD), BF16), pltpu.VMEM((heads, SB_HEAD), F32),
                        pltpu.VMEM((heads, 1), F32)],
    )
    return pl.pallas_call(
        functools.partial(_sbs_body, heads=heads, pages=pages),
        grid_spec=grid_spec,
        out_shape=jax.ShapeDtypeStruct((nb, heads, SB_HEAD), BF16),
        compiler_params=_cparams(("parallel", "arbitrary")),
        name="sb_sample",
    )(page_table, q, *([cache_k] * pages), *([cache_v] * pages), tri, bias.reshape(heads, 1), gain)


def _perm_cols(w, heads):
    lead = w.shape[:-1]
    return w.reshape(lead + (heads, RWKV_HEAD)).swapaxes(-1, -2).reshape(lead + (heads * RWKV_HEAD,))


def _unperm_cols(w, heads):
    lead = w.shape[:-1]
    return w.reshape(lead + (RWKV_HEAD, heads)).swapaxes(-1, -2).reshape(lead + (heads * RWKV_HEAD,))


def _state_in(s):
    heads = s.shape[-3]
    lead = s.shape[:-3]
    s = jnp.moveaxis(s, -3, -1)
    s = jnp.swapaxes(s, -3, -2)
    return s.reshape(lead + (RWKV_HEAD, RWKV_HEAD * heads // LANES, LANES))


def _state_out(s, heads):
    lead = s.shape[:-3]
    s = s.reshape(lead + (RWKV_HEAD, RWKV_HEAD, heads))
    s = jnp.swapaxes(s, -3, -2)
    return jnp.moveaxis(s, -1, -3)


def _layer(x, pe, shift_prev, wkv0, attend, wts, *, seq_shift):
    heads = wts["heads"]
    width = heads * RWKV_HEAD
    sb_heads = wts["sb_heads"]
    sb_width = sb_heads * SB_HEAD
    m = x.shape[0]

    h = _norm_cast(x, wts["norm_mix_pre"])
    proj = _matmul(h, wts["w_main"], name="mm_in")
    lora = _matmul(h, wts["w_lora"], name="mm_lora")

    feats_last = jnp.concatenate([_unperm_cols(proj[-1, :width], heads),
                                  _unperm_cols(proj[-1, width:2 * width], heads),
                                  _unperm_cols(proj[-1, 2 * width:3 * width], heads),
                                  lora[-1, :DECAY_LORA + AAA_LORA + GATE_LORA]])
    if seq_shift:
        tm = _tile(m, 64)
        sp = shift_prev.reshape(1, -1)
        prev_m = jnp.concatenate([sp[:, :3 * width], proj[tm - 1:m - 1:tm, :3 * width]], axis=0)
        prev_l = jnp.concatenate([sp[:, 3 * width:], lora[tm - 1:m - 1:tm]], axis=0)
        prev_m, prev_l = prev_m[:, None, :], prev_l[:, None, :]
        shift_new = feats_last[None]
    else:
        tm = m
        prev_m, prev_l = shift_prev[:, :3 * width], shift_prev[:, 3 * width:]
        shift_new = jnp.concatenate(
            [_unperm_cols(proj[:, :width], heads), _unperm_cols(proj[:, width:2 * width], heads),
             _unperm_cols(proj[:, 2 * width:3 * width], heads),
             lora[:, :DECAY_LORA + AAA_LORA + GATE_LORA]], axis=1)

    outs = _rwkv_prep(proj, lora, prev_m, prev_l, wts, seq_shift=seq_shift, heads=heads, tm=tm)
    e_list, (v_c, bonus_c, g_c) = outs[:5], outs[5:]
    v_rows = v_c.reshape(m, width // LANES, LANES)
    if seq_shift:
        y, s_fin = _wkv_scan_seq(_state_in(wkv0[0]), e_list, v_rows)
        wkv_new = _state_out(s_fin, heads)[None]
    else:
        y, s_fin = _wkv_scan_batch(_state_in(wkv0), e_list, v_rows)
        wkv_new = _state_out(s_fin, heads)
    r_out = _rwkv_post(y.reshape(m, width), bonus_c, g_c, wts, heads=heads)

    q_col = 3 * width // LANES
    a_out = attend(proj, q_col)
    kh = proj[:, 3 * width + sb_width:3 * width + 2 * sb_width].reshape(m, sb_heads, SB_HEAD)
    vh = proj[:, 3 * width + 2 * sb_width:].reshape(m, sb_heads, SB_HEAD)

    mix = _matmul_cat(r_out, a_out, wts["w_out"])
    x1, h2 = _resid_norm(x, mix, wts["norm_mix_post"], wts["norm_ffn_pre"])
    up = _matmul(h2, wts["w_up"], out_dtype=BF16, relu2=True, name="mm_up")
    f = _matmul(up, wts["w_down"], name="mm_down")
    x2, x2b = _resid_norm(x1, f, wts["norm_ffn_post"], None)
    out = _ple(x2b, wts["w_ple_gate"], x2, pe.astype(BF16), wts["w_ple_proj"])
    return out, shift_new, wkv_new, kh, vh


def _shift_in(s, width, heads):
    pad = LORA_PAD - (s.shape[1] - 3 * width)
    return jnp.concatenate([_perm_cols(s[:, :width], heads),
                            _perm_cols(s[:, width:2 * width], heads),
                            _perm_cols(s[:, 2 * width:3 * width], heads),
                            s[:, 3 * width:], jnp.zeros((s.shape[0], pad), s.dtype)], axis=1)


def kernel(x_prompt, x_sample, p_prompt, p_sample, state_rwkv_shift, state_rwkv_wkv, cache_k, cache_v, page_table, norm_mix_pre, norm_mix_post, norm_ffn_pre, norm_ffn_post, w_in, rwkv_mu, rwkv_w0, rwkv_w2, rwkv_a0, rwkv_a2, rwkv_g2, rwkv_k_k, rwkv_k_a, rwkv_r_k, rwkv_ln_w, rwkv_ln_b, sb_norm, sb_bias, w_out, w_up, w_down, w_ple_gate, w_ple_proj):
    depth = w_in.shape[0]
    heads = rwkv_r_k.shape[1]
    width = heads * RWKV_HEAD
    sb_heads = sb_bias.shape[1]
    sb_width = sb_heads * SB_HEAD
    nb, t_len, d_model = x_prompt.shape
    db = x_sample.shape[0]
    assert nb == 1 and x_sample.shape[1] == 1 and LANES == 4 * heads
    rwkv_proj = rwkv_mu.shape[1]
    n_lora = rwkv_proj - 3 * width
    assert n_lora == DECAY_LORA + AAA_LORA + GATE_LORA and sb_width + width == w_out.shape[1]

    yp, ys = x_prompt[0], x_sample[:, 0]
    res = [[] for _ in range(8)]
    for i in range(depth):
        wi = w_in[i]
        lpad = jnp.zeros((d_model, LORA_PAD - n_lora), wi.dtype)
        mu = rwkv_mu[i]
        a2p = jnp.concatenate([rwkv_a2[i], jnp.zeros((DECAY_LORA - AAA_LORA, width), F32)], axis=0)
        g2p = jnp.concatenate([jnp.zeros((AAA_LORA, width), F32), rwkv_g2[i],
                               jnp.zeros((LORA_PAD - n_lora, width), F32)], axis=0)
        wo = w_out[i]
        wts = dict(
            heads=heads, sb_heads=sb_heads,
            norm_mix_pre=norm_mix_pre[i], norm_mix_post=norm_mix_post[i],
            norm_ffn_pre=norm_ffn_pre[i], norm_ffn_post=norm_ffn_post[i],
            w_main=jnp.concatenate([_perm_cols(wi[:, :width], heads),
                                    _perm_cols(wi[:, width:2 * width], heads),
                                    _perm_cols(wi[:, 2 * width:3 * width], heads),
                                    wi[:, rwkv_proj:]], axis=1).astype(BF16),
            w_lora=jnp.concatenate([wi[:, 3 * width:rwkv_proj], lpad], axis=1).astype(BF16),
            mu_m=jnp.concatenate([_perm_cols(mu[:width], heads), _perm_cols(mu[width:2 * width], heads),
                                  _perm_cols(mu[2 * width:3 * width], heads)]).reshape(1, -1),
            mu_l=jnp.concatenate([mu[3 * width:], jnp.zeros((LORA_PAD - n_lora,), F32)]).reshape(1, -1),
            w0=_perm_cols(rwkv_w0[i], heads).reshape(1, -1),
            w2=_perm_cols(rwkv_w2[i], heads).astype(BF16),
            a0=_perm_cols(rwkv_a0[i], heads).reshape(1, -1),
            a2=_perm_cols(a2p, heads).astype(BF16),
            g2=_perm_cols(g2p, heads).astype(BF16),
            k_k=_perm_cols(rwkv_k_k[i], heads).reshape(1, -1),
            k_a=_perm_cols(rwkv_k_a[i], heads).reshape(1, -1),
            r_k=_perm_cols(rwkv_r_k[i].reshape(-1), heads).reshape(1, -1),
            ln_w=_perm_cols(rwkv_ln_w[i], heads).reshape(1, -1),
            ln_b=_perm_cols(rwkv_ln_b[i], heads).reshape(1, -1),
            w_out=jnp.concatenate([wo[:width].reshape(heads, RWKV_HEAD, -1).swapaxes(0, 1)
                                   .reshape(width, -1), wo[width:]], axis=0).astype(BF16),
            w_up=w_up[i].astype(BF16), w_down=w_down[i].astype(BF16),
            w_ple_gate=w_ple_gate[i].astype(BF16), w_ple_proj=w_ple_proj[i].astype(BF16),
        )
        bias, gain = sb_bias[i], sb_norm[i]

        attend_p = lambda proj, q_col: _sb_prompt(proj, q_col, bias, gain, heads=sb_heads)
        zero_shift = _shift_in(jnp.zeros((1, rwkv_proj), F32), width, heads)
        zero_wkv = jnp.zeros((1, heads, RWKV_HEAD, RWKV_HEAD), F32)
        yp, sp, wp, kp, vp = _layer(yp, p_prompt[i, 0], zero_shift, zero_wkv, attend_p, wts,
                                    seq_shift=True)

        n_phys = cache_k.shape[1]
        ck = cache_k.reshape((depth * n_phys,) + cache_k.shape[2:])
        cv = cache_v.reshape((depth * n_phys,) + cache_v.shape[2:])
        pages_i = page_table + i * n_phys
        attend_s = lambda proj, q_col: _sb_sample(
            proj[:, q_col * LANES:q_col * LANES + sb_width].reshape(db, sb_heads, SB_HEAD), ck, cv,
            pages_i, bias, gain, heads=sb_heads).reshape(db, sb_width)
        ys, ss, ws, kn, vn = _layer(ys, p_sample[i, :, 0], _shift_in(state_rwkv_shift[i], width, heads),
                                    state_rwkv_wkv[i], attend_s, wts, seq_shift=False)
        for lst, val in zip(res, (sp, ss, wp, ws, kp[None], vp[None], kn[:, None], vn[:, None])):
            lst.append(val)
    return (yp[None], ys[:, None]) + tuple(jnp.stack(r) for r in res)
```

```python
import functools

import jax
import jax.numpy as jnp
import numpy as np
from jax import lax
from jax.experimental import pallas as pl
from jax.experimental.pallas import tpu as pltpu

F32 = jnp.float32
BF16 = jnp.bfloat16

RMS_EPS = 1e-6
GN_EPS = 64e-5
L2_EPS = 1e-12

LANES = 128
RWKV_HEAD = 64
SB_HEAD = 128
DECAY_LORA = 128
AAA_LORA = 96
GATE_LORA = 256
LORA_PAD = 512
VMEM_LIMIT = 52 * 1024 * 1024


def _cparams(sem):
    return pltpu.CompilerParams(dimension_semantics=sem, vmem_limit_bytes=VMEM_LIMIT)


def _tile(n, pref):
    if n <= pref:
        return n
    t = pref
    while n % t:
        t //= 2
    return t


def _mm_body(a_ref, b_ref, o_ref, acc_ref, *, nk, relu2):
    def finish(acc):
        if relu2:
            acc = jnp.square(jnp.maximum(acc, 0.0))
        o_ref[...] = acc.astype(o_ref.dtype)

    if nk == 1:
        finish(jnp.dot(a_ref[...], b_ref[...], preferred_element_type=F32))
    else:
        k = pl.program_id(2)

        @pl.when(k == 0)
        def _():
            acc_ref[...] = jnp.zeros_like(acc_ref)

        acc_ref[...] += jnp.dot(a_ref[...], b_ref[...], preferred_element_type=F32)

        @pl.when(k == nk - 1)
        def _():
            finish(acc_ref[...])


def _matmul(a, b, *, out_dtype=F32, relu2=False, tm=1024, tn=1024, tk=2048, name="mm"):
    m, kdim = a.shape
    n = b.shape[1]
    tm, tn = _tile(m, tm), _tile(n, tn)
    tk = kdim if kdim <= 4096 else _tile(kdim, tk)
    nk = kdim // tk
    return pl.pallas_call(
        functools.partial(_mm_body, nk=nk, relu2=relu2),
        grid=(n // tn, m // tm, nk),
        in_specs=[pl.BlockSpec((tm, tk), lambda j, i, k: (i, k)),
                  pl.BlockSpec((tk, tn), lambda j, i, k: (k, j))],
        out_specs=pl.BlockSpec((tm, tn), lambda j, i, k: (i, j)),
        out_shape=jax.ShapeDtypeStruct((m, n), out_dtype),
        scratch_shapes=[pltpu.VMEM((tm, tn) if nk > 1 else (8, LANES), F32)],
        compiler_params=_cparams(("parallel", "parallel", "arbitrary")),
        name=name,
    )(a, b)


def _mm2_body(a1_ref, a2_ref, b1_ref, b2_ref, o_ref):
    o_ref[...] = (jnp.dot(a1_ref[...], b1_ref[...], preferred_element_type=F32)
                  + jnp.dot(a2_ref[...], b2_ref[...], preferred_element_type=F32))


def _matmul_cat(a1, a2, b, *, tm=1024, tn=1024):
    m, k1 = a1.shape
    k2 = a2.shape[1]
    n = b.shape[1]
    tm, tn = _tile(m, tm), _tile(n, tn)
    nb1 = k1 // k2
    assert k1 == nb1 * k2
    return pl.pallas_call(
        _mm2_body,
        grid=(n // tn, m // tm),
        in_specs=[pl.BlockSpec((tm, k1), lambda j, i: (i, 0)),
                  pl.BlockSpec((tm, k2), lambda j, i: (i, 0)),
                  pl.BlockSpec((k1, tn), lambda j, i: (0, j)),
                  pl.BlockSpec((k2, tn), lambda j, i: (nb1, j))],
        out_specs=pl.BlockSpec((tm, tn), lambda j, i: (i, j)),
        out_shape=jax.ShapeDtypeStruct((m, n), F32),
        compiler_params=_cparams(("parallel", "parallel")),
        name="mm_out",
    )(a1, a2, b, b)


def _ple_body(a_ref, b_ref, x_ref, pe_ref, wp_ref, o_ref):
    gate = jnp.dot(a_ref[...], b_ref[...], preferred_element_type=F32)
    proj = jnp.dot(pe_ref[...], wp_ref[...], preferred_element_type=F32)
    o_ref[...] = x_ref[...] + jax.nn.sigmoid(gate) * proj


def _ple(xb, wg, x, pe, wp, *, tm=512, tn=1024):
    m, kdim = xb.shape
    n = wg.shape[1]
    kp = pe.shape[1]
    tm, tn = _tile(m, tm), _tile(n, tn)
    return pl.pallas_call(
        _ple_body,
        grid=(n // tn, m // tm),
        in_specs=[pl.BlockSpec((tm, kdim), lambda j, i: (i, 0)),
                  pl.BlockSpec((kdim, tn), lambda j, i: (0, j)),
                  pl.BlockSpec((tm, tn), lambda j, i: (i, j)),
                  pl.BlockSpec((tm, kp), lambda j, i: (i, 0)),
                  pl.BlockSpec((kp, tn), lambda j, i: (0, j))],
        out_specs=pl.BlockSpec((tm, tn), lambda j, i: (i, j)),
        out_shape=jax.ShapeDtypeStruct((m, n), F32),
        compiler_params=_cparams(("parallel", "parallel")),
        name="ple",
    )(xb, wg, x, pe, wp)


def _rms(x, g):
    return x * lax.rsqrt(jnp.mean(x * x, axis=-1, keepdims=True) + RMS_EPS) * g


def _norm_cast_body(x_ref, g_ref, o_ref):
    o_ref[...] = _rms(x_ref[...], g_ref[...]).astype(o_ref.dtype)


def _norm_cast(x, g, *, tm=256):
    m, d = x.shape
    tm = _tile(m, tm)
    return pl.pallas_call(
        _norm_cast_body,
        grid=(m // tm,),
        in_specs=[pl.BlockSpec((tm, d), lambda i: (i, 0)),
                  pl.BlockSpec((1, d), lambda i: (0, 0))],
        out_specs=pl.BlockSpec((tm, d), lambda i: (i, 0)),
        out_shape=jax.ShapeDtypeStruct((m, d), BF16),
        compiler_params=_cparams(("parallel",)),
        name="norm_cast",
    )(x, g.reshape(1, d))


def _resid_body(x_ref, f_ref, g_ref, gn_ref, xo_ref, no_ref, *, norm_next):
    xn = x_ref[...] + _rms(f_ref[...], g_ref[...])
    xo_ref[...] = xn
    if norm_next:
        no_ref[...] = _rms(xn, gn_ref[...]).astype(no_ref.dtype)
    else:
        no_ref[...] = xn.astype(no_ref.dtype)


def _resid_norm(x, f, g, g_next, *, tm=256):
    m, d = x.shape
    tm = _tile(m, tm)
    norm_next = g_next is not None
    gn = g_next if norm_next else g
    row = pl.BlockSpec((tm, d), lambda i: (i, 0))
    vec = pl.BlockSpec((1, d), lambda i: (0, 0))
    return pl.pallas_call(
        functools.partial(_resid_body, norm_next=norm_next),
        grid=(m // tm,),
        in_specs=[row, row, vec, vec],
        out_specs=[row, row],
        out_shape=[jax.ShapeDtypeStruct((m, d), F32), jax.ShapeDtypeStruct((m, d), BF16)],
        compiler_params=_cparams(("parallel",)),
        name="resid_norm",
    )(x, f, g.reshape(1, d), gn.reshape(1, d))


def _head_allsum(p, heads):
    q = p + pltpu.roll(p, 2 * heads, axis=1)
    return q + pltpu.roll(q, heads, axis=1)


def _col_slices(x):
    return [x[:, i * LANES:(i + 1) * LANES] for i in range(x.shape[1] // LANES)]


def _split3(x):
    hi = x.astype(BF16)
    r1 = x - hi.astype(F32)
    mid = r1.astype(BF16)
    lo = (r1 - mid.astype(F32)).astype(BF16)
    return hi, mid, lo


def _group_select(heads):
    src = np.arange(LANES)[:, None]
    dst = np.arange(2 * LANES)[None, :]
    mats = [(src == (2 * p + dst // LANES) * heads + dst % heads) for p in range(2)]
    return jnp.asarray(np.stack(mats), BF16)


def _prep_body(fm_ref, fl_ref, pm_ref, plo_ref, mum_ref, mul_ref, w0_ref, w2_ref, a0_ref, a2_ref,
               g2_ref, kk_ref, ka_ref, rk_ref, sel_ref,
               e_nkk, e_d, e_b, e_k, e_r, v_out, bonus_out, g_out, *, seq_shift, heads):
    width = heads * RWKV_HEAD

    def shifted(x, p_ref):
        if not seq_shift:
            return p_ref[...]
        prev = pltpu.roll(x, 1, axis=0)
        row = lax.broadcasted_iota(jnp.int32, x.shape, 0)
        return jnp.where(row == 0, p_ref[0], prev)

    x = fm_ref[...]
    xm = x + (shifted(x, pm_ref) - x) * mum_ref[...]
    lo = fl_ref[...]
    lm = lo + (shifted(lo, plo_ref) - lo) * mul_ref[...]

    r = xm[:, :width]
    k = xm[:, width:2 * width]
    v = xm[:, 2 * width:]

    u = w0_ref[...] + jnp.dot(jnp.tanh(lm[:, :DECAY_LORA]).astype(BF16), w2_ref[...],
                              preferred_element_type=F32)
    w_log = -(jnp.maximum(-u, 0.0) + jnp.log1p(jnp.exp(-jnp.abs(u)))) - 0.5
    decay = jnp.exp(-jnp.exp(w_log))
    a = jax.nn.sigmoid(a0_ref[...] + jnp.dot(lm[:, DECAY_LORA:2 * DECAY_LORA].astype(BF16),
                                             a2_ref[...], preferred_element_type=F32))
    g = jnp.dot(jax.nn.sigmoid(lm[:, DECAY_LORA:]).astype(BF16), g2_ref[...],
                preferred_element_type=F32)

    kk = k * kk_ref[...]
    sq = _col_slices(kk * kk)
    tot = sq[0]
    for s in sq[1:]:
        tot = tot + s
    den = jnp.maximum(jnp.sqrt(_head_allsum(tot, heads)), L2_EPS)
    kk = jnp.concatenate([c / den for c in _col_slices(kk)], axis=1)
    kf = k * (1.0 + (a - 1.0) * ka_ref[...])
    b = kk * a

    rk_parts = _col_slices(r * kf * rk_ref[...])
    rk_tot = rk_parts[0]
    for s in rk_parts[1:]:
        rk_tot = rk_tot + s
    rk_tot = _head_allsum(rk_tot, heads)
    v_out[...] = v
    bonus_out[...] = jnp.concatenate([rk_tot * c for c in _col_slices(v)], axis=1)
    g_out[...] = g

    rows = x.shape[0]

    def expand(val, ref):
        parts = _split3(jnp.concatenate(_col_slices(val), axis=0))
        for p in range(2):
            sel = sel_ref[p]
            out = (jnp.dot(parts[0], sel, preferred_element_type=F32)
                   + jnp.dot(parts[1], sel, preferred_element_type=F32)
                   + jnp.dot(parts[2], sel, preferred_element_type=F32))
            for i in range(width // LANES):
                for q in range(2):
                    ref[4 * i + 2 * p + q] = out[i * rows:(i + 1) * rows, q * LANES:(q + 1) * LANES]

    expand(-kk, e_nkk)
    expand(decay, e_d)
    expand(b, e_b)
    expand(kf, e_k)
    expand(r, e_r)


def _rwkv_prep(feats, lora, prev_m, prev_l, prm, *, seq_shift, heads, tm):
    m = feats.shape[0]
    width = heads * RWKV_HEAD
    tm = _tile(m, tm)
    nt = m // tm
    if seq_shift:
        pm_spec = pl.BlockSpec((1, 1, 3 * width), lambda i: (i, 0, 0))
        pl_spec = pl.BlockSpec((1, 1, LORA_PAD), lambda i: (i, 0, 0))
    else:
        pm_spec = pl.BlockSpec((tm, 3 * width), lambda i: (i, 0))
        pl_spec = pl.BlockSpec((tm, LORA_PAD), lambda i: (i, 0))

    def vec(n):
        return pl.BlockSpec((1, n), lambda i: (0, 0))

    def full(r_, c_):
        return pl.BlockSpec((r_, c_), lambda i: (0, 0))

    e_spec = pl.BlockSpec((RWKV_HEAD, tm, LANES), lambda i: (0, i, 0))
    c_spec = pl.BlockSpec((tm, width), lambda i: (i, 0))
    e_shape = jax.ShapeDtypeStruct((RWKV_HEAD, m, LANES), F32)
    c_shape = jax.ShapeDtypeStruct((m, width), F32)
    return pl.pallas_call(
        functools.partial(_prep_body, seq_shift=seq_shift, heads=heads),
        grid=(nt,),
        in_specs=[pl.BlockSpec((tm, 3 * width), lambda i: (i, 0)),
                  pl.BlockSpec((tm, LORA_PAD), lambda i: (i, 0)),
                  pm_spec, pl_spec, vec(3 * width), vec(LORA_PAD),
                  vec(width), full(DECAY_LORA, width), vec(width), full(DECAY_LORA, width),
                  full(LORA_PAD - DECAY_LORA, width), vec(width), vec(width), vec(width),
                  pl.BlockSpec((2, LANES, 2 * LANES), lambda i: (0, 0, 0))],
        out_specs=[e_spec] * 5 + [c_spec] * 3,
        out_shape=[e_shape] * 5 + [c_shape] * 3,
        compiler_params=_cparams(("parallel",)),
        name="rwkv_prep",
    )(feats, lora, prev_m, prev_l, prm["mu_m"], prm["mu_l"], prm["w0"], prm["w2"], prm["a0"],
      prm["a2"], prm["g2"], prm["k_k"], prm["k_a"], prm["r_k"], _group_select(heads))


N_ACC = 4


def _tree_sum(parts):
    while len(parts) > 1:
        parts = [parts[i] + parts[i + 1] for i in range(0, len(parts), 2)]
    return parts[0]


def _wkv_sa(s_ref, nkk_ref, t):
    acc = [None] * N_ACC
    for kx in range(RWKV_HEAD):
        term = s_ref[kx] * nkk_ref[kx, pl.ds(t, 1), :]
        acc[kx % N_ACC] = term if acc[kx % N_ACC] is None else acc[kx % N_ACC] + term
    return _tree_sum(acc)


def _wkv_step(s_ref, nkk_ref, d_ref, b_ref, k_ref, r_ref, v_t, sa, t, t_next):
    yacc = [None] * N_ACC
    sacc = [None] * N_ACC
    for kx in range(RWKV_HEAD):
        s_new = (s_ref[kx] * d_ref[kx, pl.ds(t, 1), :] + sa * b_ref[kx, pl.ds(t, 1), :]
                 + v_t * k_ref[kx, pl.ds(t, 1), :])
        s_ref[kx] = s_new
        term = s_new * r_ref[kx, pl.ds(t, 1), :]
        yacc[kx % N_ACC] = term if yacc[kx % N_ACC] is None else yacc[kx % N_ACC] + term
        if t_next is not None:
            term = s_new * nkk_ref[kx, pl.ds(t_next, 1), :]
            sacc[kx % N_ACC] = term if sacc[kx % N_ACC] is None else sacc[kx % N_ACC] + term
    return _tree_sum(yacc), (None if t_next is None else _tree_sum(sacc))


def _scan_seq_body(s0_ref, nkk_ref, d_ref, b_ref, k_ref, r_ref, v_ref, y_ref, sout_ref, s_ref, *,
                   tc):
    c = pl.program_id(0)

    @pl.when(c == 0)
    def _():
        s_ref[...] = s0_ref[...]

    def step(t, sa):
        y_ref[t], sa_next = _wkv_step(s_ref, nkk_ref, d_ref, b_ref, k_ref, r_ref, v_ref[t], sa, t,
                                      jnp.minimum(t + 1, tc - 1))
        return sa_next

    lax.fori_loop(0, tc, step, _wkv_sa(s_ref, nkk_ref, 0), unroll=4)

    @pl.when(c == pl.num_programs(0) - 1)
    def _():
        sout_ref[...] = s_ref[...]


def _wkv_scan_seq(s0, e_list, v, *, tc=64):
    t_len = v.shape[0]
    tc = _tile(t_len, tc)
    s_spec = pl.BlockSpec(s0.shape, lambda c: (0, 0, 0))
    e_spec = pl.BlockSpec((RWKV_HEAD, tc, LANES), lambda c: (0, c, 0))
    v_spec = pl.BlockSpec((tc,) + v.shape[1:], lambda c: (c, 0, 0))
    return pl.pallas_call(
        functools.partial(_scan_seq_body, tc=tc),
        grid=(t_len // tc,),
        in_specs=[s_spec] + [e_spec] * 5 + [v_spec],
        out_specs=[v_spec, s_spec],
        out_shape=[jax.ShapeDtypeStruct(v.shape, F32), jax.ShapeDtypeStruct(s0.shape, F32)],
        scratch_shapes=[pltpu.VMEM(s0.shape, F32)],
        compiler_params=_cparams(("arbitrary",)),
        name="wkv_scan",
    )(s0, *e_list, v)


def _scan_batch_body(s0_ref, nkk_ref, d_ref, b_ref, k_ref, r_ref, v_ref, y_ref, sout_ref):
    bidx = pl.program_id(0)
    s = sout_ref.at[0]
    s[...] = s0_ref[0]
    y_ref[0], _ = _wkv_step(s, nkk_ref, d_ref, b_ref, k_ref, r_ref, v_ref[0],
                            _wkv_sa(s, nkk_ref, bidx), bidx, None)


def _wkv_scan_batch(s0, e_list, v):
    nb = v.shape[0]
    s_spec = pl.BlockSpec((1,) + s0.shape[1:], lambda b: (b, 0, 0, 0))
    e_spec = pl.BlockSpec((RWKV_HEAD, nb, LANES), lambda b: (0, 0, 0))
    v_spec = pl.BlockSpec((1,) + v.shape[1:], lambda b: (b, 0, 0))
    return pl.pallas_call(
        _scan_batch_body,
        grid=(nb,),
        in_specs=[s_spec] + [e_spec] * 5 + [v_spec],
        out_specs=[v_spec, s_spec],
        out_shape=[jax.ShapeDtypeStruct(v.shape, F32), jax.ShapeDtypeStruct(s0.shape, F32)],
        compiler_params=_cparams(("arbitrary",)),
        name="wkv_step",
    )(s0, *e_list, v)


def _post_body(y_ref, bonus_ref, g_ref, lnw_ref, lnb_ref, o_ref, *, heads):
    inv_n = 1.0 / RWKV_HEAD

    def head_sum(x):
        cols = _col_slices(x)
        tot = cols[0]
        for c in cols[1:]:
            tot = tot + c
        return _head_allsum(tot, heads)

    def tiled(stat, like):
        return jnp.concatenate([stat] * (like.shape[1] // LANES), axis=1)

    y = y_ref[...]
    yc = y - tiled(head_sum(y) * inv_n, y)
    var = head_sum(yc * yc) * inv_n
    yn = yc * tiled(lax.rsqrt(var + GN_EPS), y) * lnw_ref[...] + lnb_ref[...]
    o_ref[...] = ((yn + bonus_ref[...]) * g_ref[...]).astype(o_ref.dtype)


def _rwkv_post(y, bonus, g, prm, *, heads, tm=256):
    m, width = y.shape
    tm = _tile(m, tm)
    row = pl.BlockSpec((tm, width), lambda i: (i, 0))
    vec = pl.BlockSpec((1, width), lambda i: (0, 0))
    return pl.pallas_call(
        functools.partial(_post_body, heads=heads),
        grid=(m // tm,),
        in_specs=[row] * 3 + [vec] * 2,
        out_specs=row,
        out_shape=jax.ShapeDtypeStruct((m, width), BF16),
        compiler_params=_cparams(("parallel",)),
        name="rwkv_post",
    )(y, bonus, g, prm["ln_w"], prm["ln_b"])


SB_TILE = 512
SB_SUB = 256
SB_HEADS_PER_STEP = 4
LOG2E = 1.4426950408889634


def _log_sigmoid_pair(z):
    lsz = jnp.minimum(z, 0.0) - jnp.log(1.0 + jnp.exp2(jnp.abs(z) * (-LOG2E)))
    return lsz, lsz - z


def _sb_weights(lsz, lsn, n_blocks, tri2_ref, carry, valid):
    rows = lsz.shape[0] // n_blocks
    hi = lsn.astype(BF16)
    lo = (lsn - hi.astype(F32)).astype(BF16)
    if n_blocks == 1:
        tri = tri2_ref[:lsz.shape[1]]
        logw = lsz + (jnp.dot(hi, tri, preferred_element_type=F32)
                      + jnp.dot(lo, tri, preferred_element_type=F32))
    else:
        logw = lsz + jnp.dot(jnp.concatenate([hi, lo], axis=1), tri2_ref[...],
                             preferred_element_type=F32)
    taken = jnp.sum(lsn, axis=1, keepdims=True)
    weights = []
    for b in range(n_blocks):
        sl = slice(b * rows, (b + 1) * rows)
        w = jnp.exp(logw[sl] + carry)
        if valid is not None:
            w = jnp.where(valid[sl], w, 0.0)
        weights.append(w.astype(BF16))
        carry = carry + taken[sl]
    return weights, carry


def _tri2(n):
    tri = (lax.broadcasted_iota(jnp.int32, (n, n), 0)
           > lax.broadcasted_iota(jnp.int32, (n, n), 1)).astype(BF16)
    return jnp.concatenate([tri, tri], axis=0)


def _suffix_sum(ls, tri_ref):
    hi = ls.astype(BF16)
    lo = (ls - hi.astype(F32)).astype(BF16)
    tri = tri_ref[...]
    return (jnp.dot(hi, tri, preferred_element_type=F32)
            + jnp.dot(lo, tri, preferred_element_type=F32))


def _sbp_body(qi_ref, kb_ref, bias_ref, q_ref, k_ref, v_ref, tri_ref, g_ref,
              o_ref, qs_ref, acc_ref, carry_ref):
    hp = pl.program_id(0)
    s = pl.program_id(1)
    tile = q_ref.shape[0]
    n_sub = tri_ref.shape[0]
    n_heads = q_ref.shape[1] // SB_HEAD
    diagonal = kb_ref[s] == qi_ref[s]

    def visit(masked):
        if masked:
            causal = (lax.broadcasted_iota(jnp.int32, (tile, tile), 1)
                      < lax.broadcasted_iota(jnp.int32, (tile, tile), 0))
        for hh in range(n_heads):
            cols = slice(hh * SB_HEAD, (hh + 1) * SB_HEAD)
            kmat = k_ref[:, cols].astype(BF16)
            vmat = v_ref[:, cols].astype(BF16)
            z = lax.dot_general(qs_ref[:, cols], kmat, (((1,), (1,)), ((), ())),
                                preferred_element_type=F32) + bias_ref[hp * n_heads + hh]
            lsz, lsn = _log_sigmoid_pair(z)
            if masked:
                lsn = jnp.where(causal, lsn, 0.0)
            carry = carry_ref[hh]
            w_parts = [None] * (tile // n_sub)
            for sub in reversed(range(tile // n_sub)):
                sl = slice(sub * n_sub, (sub + 1) * n_sub)
                w = jnp.exp(lsz[:, sl] + _suffix_sum(lsn[:, sl], tri_ref) + carry)
                if masked:
                    w = jnp.where(causal[:, sl], w, 0.0)
                w_parts[sub] = w.astype(BF16)
                carry = carry + jnp.sum(lsn[:, sl], axis=1, keepdims=True)
            carry_ref[hh] = carry
            acc_ref[:, cols] += jnp.dot(jnp.concatenate(w_parts, axis=1), vmat,
                                        preferred_element_type=F32)

    @pl.when(diagonal)
    def _():
        qs_ref[...] = (q_ref[...] * (SB_HEAD ** -0.5)).astype(BF16)
        acc_ref[...] = jnp.zeros_like(acc_ref)
        carry_ref[...] = jnp.zeros_like(carry_ref)
        visit(True)

    @pl.when(jnp.logical_not(diagonal))
    def _():
        visit(False)

    @pl.when(kb_ref[s] == 0)
    def _():
        for hh in range(n_heads):
            cols = slice(hh * SB_HEAD, (hh + 1) * SB_HEAD)
            o_ref[:, cols] = _rms(acc_ref[:, cols], g_ref[hh]).astype(o_ref.dtype)


def _sb_schedule(n_tiles):
    qi = [i for i in range(n_tiles) for _ in range(i + 1)]
    kb = [j for i in range(n_tiles) for j in range(i, -1, -1)]
    return [jnp.asarray(np.asarray(a, np.int32)) for a in (qi, kb)]


def _sb_prompt(qkv, q_col, bias, gain, *, heads):
    t_len = qkv.shape[0]
    tile = _tile(t_len, SB_TILE)
    sched = _sb_schedule(t_len // tile)
    nsteps = sched[0].shape[0]
    sub = min(SB_SUB, tile)
    hps = SB_HEADS_PER_STEP
    assert heads % hps == 0 and q_col % hps == 0
    wide = hps * SB_HEAD
    qc = q_col // hps
    grid_spec = pltpu.PrefetchScalarGridSpec(
        num_scalar_prefetch=3,
        grid=(heads // hps, nsteps),
        in_specs=[
            pl.BlockSpec((tile, wide), lambda h, s, qi, kb, b: (qi[s], qc + h)),
            pl.BlockSpec((tile, wide), lambda h, s, qi, kb, b: (kb[s], qc + heads // hps + h)),
            pl.BlockSpec((tile, wide), lambda h, s, qi, kb, b: (kb[s], qc + 2 * (heads // hps) + h)),
            pl.BlockSpec((sub, sub), lambda h, s, qi, kb, b: (0, 0)),
            pl.BlockSpec((hps, 1, SB_HEAD), lambda h, s, qi, kb, b: (h, 0, 0)),
        ],
        out_specs=pl.BlockSpec((tile, wide), lambda h, s, qi, kb, b: (qi[s], h)),
        scratch_shapes=[pltpu.VMEM((tile, wide), BF16), pltpu.VMEM((tile, wide), F32),
                        pltpu.VMEM((hps, tile, 1), F32)],
    )
    return pl.pallas_call(
        _sbp_body,
        grid_spec=grid_spec,
        out_shape=jax.ShapeDtypeStruct((t_len, heads * SB_HEAD), BF16),
        compiler_params=_cparams(("parallel", "arbitrary")),
        name="sb_prompt",
    )(*sched, bias, qkv, qkv, qkv, _tri2(sub)[:sub], gain.reshape(heads, 1, SB_HEAD))


SBS_PAGES = 4


def _sbs_body(pt_ref, q_ref, *refs, heads, pages):
    k_refs, v_refs = refs[:pages], refs[pages:2 * pages]
    tri_ref, bias_ref, g_ref, o_ref, qs_ref, acc_ref, carry_ref = refs[2 * pages:]
    p = pl.program_id(1)
    rows = k_refs[0].shape[1] * heads
    n_sub = tri_ref.shape[1]
    subs = list(reversed(range(rows // n_sub)))

    @pl.when(p == 0)
    def _():
        qs_ref[...] = (q_ref[0] * (SB_HEAD ** -0.5)).astype(BF16)
        acc_ref[...] = jnp.zeros_like(acc_ref)
        carry_ref[...] = jnp.zeros_like(carry_ref)

    z2_blocks = []
    for k_ref in k_refs:
        kmat = k_ref[0].reshape(rows, SB_HEAD).astype(BF16)
        z2 = lax.dot_general(qs_ref[...], kmat, (((1,), (1,)), ((), ())),
                             preferred_element_type=F32) + bias_ref[...]
        z2_blocks += [z2[:, u * n_sub:(u + 1) * n_sub] for u in subs]
    stacked = (len(z2_blocks) * heads, n_sub)
    own_head = (lax.broadcasted_iota(jnp.int32, stacked, 1) % heads
                == lax.broadcasted_iota(jnp.int32, stacked, 0) % heads)
    lsz, lsn = _log_sigmoid_pair(jnp.concatenate(z2_blocks, axis=0))
    weights, carry_ref[...] = _sb_weights(lsz, jnp.where(own_head, lsn, 0.0), len(z2_blocks),
                                          tri_ref, carry_ref[...], own_head)
    acc = acc_ref[...]
    for u, v_ref in enumerate(v_refs):
        vmat = v_ref[0].reshape(rows, SB_HEAD).astype(BF16)
        w_page = weights[u * len(subs):(u + 1) * len(subs)][::-1]
        acc = acc + jnp.dot(jnp.concatenate(w_page, axis=1), vmat, preferred_element_type=F32)
    acc_ref[...] = acc

    @pl.when(p == pl.num_programs(1) - 1)
    def _():
        o_ref[0] = _rms(acc_ref[...], g_ref[...]).astype(o_ref.dtype)


def _sb_sample(q, cache_k, cache_v, page_table, bias, gain, *, heads):
    nb = q.shape[0]
    n_pages = page_table.shape[1]
    page = cache_k.shape[1]
    sub = min(SB_SUB, page * heads)
    pages = SBS_PAGES
    while n_pages % pages:
        pages //= 2

    def kv_spec(u):
        return pl.BlockSpec((1, page, heads, SB_HEAD),
                            lambda b, p, pt: (pt[b, n_pages - 1 - (p * pages + u)], 0, 0, 0))

    kv_specs = [kv_spec(u) for u in range(pages)]
    grid_spec = pltpu.PrefetchScalarGridSpec(
        num_scalar_prefetch=1,
        grid=(nb, n_pages // pages),
        in_specs=[pl.BlockSpec((1, heads, SB_HEAD), lambda b, p, pt: (b, 0, 0))]
                 + kv_specs + kv_specs
                 + [pl.BlockSpec((2 * sub, sub), lambda b, p, pt: (0, 0)),
                    pl.BlockSpec((heads, 1), lambda b, p, pt: (0, 0)),
                    pl.BlockSpec((heads, SB_HEAD), lambda b, p, pt: (0, 0))],
        out_specs=pl.BlockSpec((1, heads, SB_HEAD), lambda b, p, pt: (b, 0, 0)),
        scratch_shapes=[pltpu.VMEM((heads, SB_HEAD), BF16), pltpu.VMEM((heads, SB_HEAD), F32),
                        pltpu.VMEM((heads, 1), F32)],
    )
    return pl.pallas_call(
        functools.partial(_sbs_body, heads=heads, pages=pages),
        grid_spec=grid_spec,
        out_shape=jax.ShapeDtypeStruct((nb, heads, SB_HEAD), BF16),
        compiler_params=_cparams(("parallel", "arbitrary")),
        name="sb_sample",
    )(page_table, q, *([cache_k] * pages), *([cache_v] * pages), _tri2(sub),
      bias.reshape(heads, 1), gain)


def _perm_cols(w, heads):
    lead = w.shape[:-1]
    return w.reshape(lead + (heads, RWKV_HEAD)).swapaxes(-1, -2).reshape(lead + (heads * RWKV_HEAD,))


def _unperm_cols(w, heads):
    lead = w.shape[:-1]
    return w.reshape(lead + (RWKV_HEAD, heads)).swapaxes(-1, -2).reshape(lead + (heads * RWKV_HEAD,))


def _state_in(s):
    heads = s.shape[-3]
    lead = s.shape[:-3]
    s = jnp.moveaxis(s, -3, -1)
    s = jnp.swapaxes(s, -3, -2)
    return s.reshape(lead + (RWKV_HEAD, RWKV_HEAD * heads // LANES, LANES))


def _state_out(s, heads):
    lead = s.shape[:-3]
    s = s.reshape(lead + (RWKV_HEAD, RWKV_HEAD, heads))
    s = jnp.swapaxes(s, -3, -2)
    return jnp.moveaxis(s, -1, -3)


def _layer(x, pe, shift_prev, wkv0, attend, wts, *, seq_shift):
    heads = wts["heads"]
    width = heads * RWKV_HEAD
    sb_heads = wts["sb_heads"]
    sb_width = sb_heads * SB_HEAD
    m = x.shape[0]

    h = _norm_cast(x, wts["norm_mix_pre"])
    proj = _matmul(h, wts["w_main"], name="mm_in")
    lora = _matmul(h, wts["w_lora"], name="mm_lora")

    feats_last = jnp.concatenate([_unperm_cols(proj[-1, :width], heads),
                                  _unperm_cols(proj[-1, width:2 * width], heads),
                                  _unperm_cols(proj[-1, 2 * width:3 * width], heads),
                                  lora[-1, :DECAY_LORA + AAA_LORA + GATE_LORA]])
    if seq_shift:
        tm = _tile(m, 64)
        sp = shift_prev.reshape(1, -1)
        prev_m = jnp.concatenate([sp[:, :3 * width], proj[tm - 1:m - 1:tm, :3 * width]], axis=0)
        prev_l = jnp.concatenate([sp[:, 3 * width:], lora[tm - 1:m - 1:tm]], axis=0)
        prev_m, prev_l = prev_m[:, None, :], prev_l[:, None, :]
        shift_new = feats_last[None]
    else:
        tm = m
        prev_m, prev_l = shift_prev[:, :3 * width], shift_prev[:, 3 * width:]
        shift_new = jnp.concatenate(
            [_unperm_cols(proj[:, :width], heads), _unperm_cols(proj[:, width:2 * width], heads),
             _unperm_cols(proj[:, 2 * width:3 * width], heads),
             lora[:, :DECAY_LORA + AAA_LORA + GATE_LORA]], axis=1)

    outs = _rwkv_prep(proj, lora, prev_m, prev_l, wts, seq_shift=seq_shift, heads=heads, tm=tm)
    e_list, (v_c, bonus_c, g_c) = outs[:5], outs[5:]
    v_rows = v_c.reshape(m, width // LANES, LANES)
    if seq_shift:
        y, s_fin = _wkv_scan_seq(_state_in(wkv0[0]), e_list, v_rows)
        wkv_new = _state_out(s_fin, heads)[None]
    else:
        y, s_fin = _wkv_scan_batch(_state_in(wkv0), e_list, v_rows)
        wkv_new = _state_out(s_fin, heads)
    r_out = _rwkv_post(y.reshape(m, width), bonus_c, g_c, wts, heads=heads)

    q_col = 3 * width // LANES
    a_out = attend(proj, q_col)
    kh = proj[:, 3 * width + sb_width:3 * width + 2 * sb_width].reshape(m, sb_heads, SB_HEAD)
    vh = proj[:, 3 * width + 2 * sb_width:].reshape(m, sb_heads, SB_HEAD)

    mix = _matmul_cat(r_out, a_out, wts["w_out"])
    x1, h2 = _resid_norm(x, mix, wts["norm_mix_post"], wts["norm_ffn_pre"])
    up = _matmul(h2, wts["w_up"], out_dtype=BF16, relu2=True, name="mm_up")
    f = _matmul(up, wts["w_down"], name="mm_down")
    x2, x2b = _resid_norm(x1, f, wts["norm_ffn_post"], None)
    out = _ple(x2b, wts["w_ple_gate"], x2, pe.astype(BF16), wts["w_ple_proj"])
    return out, shift_new, wkv_new, kh, vh


def _shift_in(s, width, heads):
    pad = LORA_PAD - (s.shape[1] - 3 * width)
    return jnp.concatenate([_perm_cols(s[:, :width], heads),
                            _perm_cols(s[:, width:2 * width], heads),
                            _perm_cols(s[:, 2 * width:3 * width], heads),
                            s[:, 3 * width:], jnp.zeros((s.shape[0], pad), s.dtype)], axis=1)


def kernel(x_prompt, x_sample, p_prompt, p_sample, state_rwkv_shift, state_rwkv_wkv, cache_k, cache_v, page_table, norm_mix_pre, norm_mix_post, norm_ffn_pre, norm_ffn_post, w_in, rwkv_mu, rwkv_w0, rwkv_w2, rwkv_a0, rwkv_a2, rwkv_g2, rwkv_k_k, rwkv_k_a, rwkv_r_k, rwkv_ln_w, rwkv_ln_b, sb_norm, sb_bias, w_out, w_up, w_down, w_ple_gate, w_ple_proj):
    depth = w_in.shape[0]
    heads = rwkv_r_k.shape[1]
    width = heads * RWKV_HEAD
    sb_heads = sb_bias.shape[1]
    sb_width = sb_heads * SB_HEAD
    nb, t_len, d_model = x_prompt.shape
    db = x_sample.shape[0]
    assert nb == 1 and x_sample.shape[1] == 1 and LANES == 4 * heads
    rwkv_proj = rwkv_mu.shape[1]
    n_lora = rwkv_proj - 3 * width
    assert n_lora == DECAY_LORA + AAA_LORA + GATE_LORA and sb_width + width == w_out.shape[1]

    yp, ys = x_prompt[0], x_sample[:, 0]
    res = [[] for _ in range(8)]
    for i in range(depth):
        wi = w_in[i]
        lpad = jnp.zeros((d_model, LORA_PAD - n_lora), wi.dtype)
        mu = rwkv_mu[i]
        a2p = jnp.concatenate([rwkv_a2[i], jnp.zeros((DECAY_LORA - AAA_LORA, width), F32)], axis=0)
        g2p = jnp.concatenate([jnp.zeros((AAA_LORA, width), F32), rwkv_g2[i],
                               jnp.zeros((LORA_PAD - n_lora, width), F32)], axis=0)
        wo = w_out[i]
        wts = dict(
            heads=heads, sb_heads=sb_heads,
            norm_mix_pre=norm_mix_pre[i], norm_mix_post=norm_mix_post[i],
            norm_ffn_pre=norm_ffn_pre[i], norm_ffn_post=norm_ffn_post[i],
            w_main=jnp.concatenate([_perm_cols(wi[:, :width], heads),
                                    _perm_cols(wi[:, width:2 * width], heads),
                                    _perm_cols(wi[:, 2 * width:3 * width], heads),
                                    wi[:, rwkv_proj:]], axis=1).astype(BF16),
            w_lora=jnp.concatenate([wi[:, 3 * width:rwkv_proj], lpad], axis=1).astype(BF16),
            mu_m=jnp.concatenate([_perm_cols(mu[:width], heads), _perm_cols(mu[width:2 * width], heads),
                                  _perm_cols(mu[2 * width:3 * width], heads)]).reshape(1, -1),
            mu_l=jnp.concatenate([mu[3 * width:], jnp.zeros((LORA_PAD - n_lora,), F32)]).reshape(1, -1),
            w0=_perm_cols(rwkv_w0[i], heads).reshape(1, -1),
            w2=_perm_cols(rwkv_w2[i], heads).astype(BF16),
            a0=_perm_cols(rwkv_a0[i], heads).reshape(1, -1),
            a2=_perm_cols(a2p, heads).astype(BF16),
            g2=_perm_cols(g2p, heads).astype(BF16),
            k_k=_perm_cols(rwkv_k_k[i], heads).reshape(1, -1),
            k_a=_perm_cols(rwkv_k_a[i], heads).reshape(1, -1),
            r_k=_perm_cols(rwkv_r_k[i].reshape(-1), heads).reshape(1, -1),
            ln_w=_perm_cols(rwkv_ln_w[i], heads).reshape(1, -1),
            ln_b=_perm_cols(rwkv_ln_b[i], heads).reshape(1, -1),
            w_out=jnp.concatenate([wo[:width].reshape(heads, RWKV_HEAD, -1).swapaxes(0, 1)
                                   .reshape(width, -1), wo[width:]], axis=0).astype(BF16),
            w_up=w_up[i].astype(BF16), w_down=w_down[i].astype(BF16),
            w_ple_gate=w_ple_gate[i].astype(BF16), w_ple_proj=w_ple_proj[i].astype(BF16),
        )
        bias, gain = sb_bias[i], sb_norm[i]

        attend_p = lambda proj, q_col: _sb_prompt(proj, q_col, bias, gain, heads=sb_heads)
        zero_shift = _shift_in(jnp.zeros((1, rwkv_proj), F32), width, heads)
        zero_wkv = jnp.zeros((1, heads, RWKV_HEAD, RWKV_HEAD), F32)
        yp, sp, wp, kp, vp = _layer(yp, p_prompt[i, 0], zero_shift, zero_wkv, attend_p, wts,
                                    seq_shift=True)

        n_phys = cache_k.shape[1]
        ck = cache_k.reshape((depth * n_phys,) + cache_k.shape[2:])
        cv = cache_v.reshape((depth * n_phys,) + cache_v.shape[2:])
        pages_i = page_table + i * n_phys
        attend_s = lambda proj, q_col: _sb_sample(
            proj[:, q_col * LANES:q_col * LANES + sb_width].reshape(db, sb_heads, SB_HEAD), ck, cv,
            pages_i, bias, gain, heads=sb_heads).reshape(db, sb_width)
        ys, ss, ws, kn, vn = _layer(ys, p_sample[i, :, 0], _shift_in(state_rwkv_shift[i], width, heads),
                                    state_rwkv_wkv[i], attend_s, wts, seq_shift=False)
        for lst, val in zip(res, (sp, ss, wp, ws, kp[None], vp[None], kn[:, None], vn[:, None])):
            lst.append(val)
    return (yp[None], ys[:, None]) + tuple(jnp.stack(r) for r in res)
```

```python
import functools

import jax
import jax.numpy as jnp
import numpy as np
from jax import lax
from jax.experimental import pallas as pl
from jax.experimental.pallas import tpu as pltpu

F32 = jnp.float32
BF16 = jnp.bfloat16

RMS_EPS = 1e-6
GN_EPS = 64e-5
L2_EPS = 1e-12

LANES = 128
RWKV_HEAD = 64
SB_HEAD = 128
DECAY_LORA = 128
AAA_LORA = 96
GATE_LORA = 256
LORA_PAD = 512
VMEM_LIMIT = 52 * 1024 * 1024


def _cparams(sem):
    return pltpu.CompilerParams(dimension_semantics=sem, vmem_limit_bytes=VMEM_LIMIT)


def _tile(n, pref):
    if n <= pref:
        return n
    t = pref
    while n % t:
        t //= 2
    return t


def _mm_body(a_ref, b_ref, o_ref, acc_ref, *, nk, relu2):
    def finish(acc):
        if relu2:
            acc = jnp.square(jnp.maximum(acc, 0.0))
        o_ref[...] = acc.astype(o_ref.dtype)

    if nk == 1:
        finish(jnp.dot(a_ref[...], b_ref[...], preferred_element_type=F32))
    else:
        k = pl.program_id(2)

        @pl.when(k == 0)
        def _():
            acc_ref[...] = jnp.zeros_like(acc_ref)

        acc_ref[...] += jnp.dot(a_ref[...], b_ref[...], preferred_element_type=F32)

        @pl.when(k == nk - 1)
        def _():
            finish(acc_ref[...])


def _matmul(a, b, *, out_dtype=F32, relu2=False, tm=1024, tn=1024, tk=2048, name="mm"):
    m, kdim = a.shape
    n = b.shape[1]
    tm, tn = _tile(m, tm), _tile(n, tn)
    tk = kdim if kdim <= 4096 else _tile(kdim, tk)
    nk = kdim // tk
    return pl.pallas_call(
        functools.partial(_mm_body, nk=nk, relu2=relu2),
        grid=(n // tn, m // tm, nk),
        in_specs=[pl.BlockSpec((tm, tk), lambda j, i, k: (i, k)),
                  pl.BlockSpec((tk, tn), lambda j, i, k: (k, j))],
        out_specs=pl.BlockSpec((tm, tn), lambda j, i, k: (i, j)),
        out_shape=jax.ShapeDtypeStruct((m, n), out_dtype),
        scratch_shapes=[pltpu.VMEM((tm, tn) if nk > 1 else (8, LANES), F32)],
        compiler_params=_cparams(("parallel", "parallel", "arbitrary")),
        name=name,
    )(a, b)


def _mm2_body(a1_ref, a2_ref, b1_ref, b2_ref, o_ref):
    o_ref[...] = (jnp.dot(a1_ref[...], b1_ref[...], preferred_element_type=F32)
                  + jnp.dot(a2_ref[...], b2_ref[...], preferred_element_type=F32))


def _matmul_cat(a1, a2, b1, b2, *, tm=1024, tn=1024):
    m, k1 = a1.shape
    k2 = a2.shape[1]
    n = b1.shape[1]
    tm, tn = _tile(m, tm), _tile(n, tn)
    return pl.pallas_call(
        _mm2_body,
        grid=(n // tn, m // tm),
        in_specs=[pl.BlockSpec((tm, k1), lambda j, i: (i, 0)),
                  pl.BlockSpec((tm, k2), lambda j, i: (i, 0)),
                  pl.BlockSpec((k1, tn), lambda j, i: (0, j)),
                  pl.BlockSpec((k2, tn), lambda j, i: (0, j))],
        out_specs=pl.BlockSpec((tm, tn), lambda j, i: (i, j)),
        out_shape=jax.ShapeDtypeStruct((m, n), F32),
        compiler_params=_cparams(("parallel", "parallel")),
        name="mm_out",
    )(a1, a2, b1, b2)


def _ple_body(a_ref, b_ref, x_ref, pe_ref, wp_ref, o_ref):
    gate = jnp.dot(a_ref[...], b_ref[...], preferred_element_type=F32)
    proj = jnp.dot(pe_ref[...], wp_ref[...], preferred_element_type=F32)
    o_ref[...] = x_ref[...] + jax.nn.sigmoid(gate) * proj


def _ple(xb, wg, x, pe, wp, *, tm=512, tn=1024):
    m, kdim = xb.shape
    n = wg.shape[1]
    kp = pe.shape[1]
    tm, tn = _tile(m, tm), _tile(n, tn)
    return pl.pallas_call(
        _ple_body,
        grid=(n // tn, m // tm),
        in_specs=[pl.BlockSpec((tm, kdim), lambda j, i: (i, 0)),
                  pl.BlockSpec((kdim, tn), lambda j, i: (0, j)),
                  pl.BlockSpec((tm, tn), lambda j, i: (i, j)),
                  pl.BlockSpec((tm, kp), lambda j, i: (i, 0)),
                  pl.BlockSpec((kp, tn), lambda j, i: (0, j))],
        out_specs=pl.BlockSpec((tm, tn), lambda j, i: (i, j)),
        out_shape=jax.ShapeDtypeStruct((m, n), F32),
        compiler_params=_cparams(("parallel", "parallel")),
        name="ple",
    )(xb, wg, x, pe, wp)


def _rms(x, g):
    return x * lax.rsqrt(jnp.mean(x * x, axis=-1, keepdims=True) + RMS_EPS) * g


def _norm_cast_body(x_ref, g_ref, o_ref):
    o_ref[...] = _rms(x_ref[...], g_ref[...]).astype(o_ref.dtype)


def _norm_cast(x, g, *, tm=256):
    m, d = x.shape
    tm = _tile(m, tm)
    return pl.pallas_call(
        _norm_cast_body,
        grid=(m // tm,),
        in_specs=[pl.BlockSpec((tm, d), lambda i: (i, 0)),
                  pl.BlockSpec((1, d), lambda i: (0, 0))],
        out_specs=pl.BlockSpec((tm, d), lambda i: (i, 0)),
        out_shape=jax.ShapeDtypeStruct((m, d), BF16),
        compiler_params=_cparams(("parallel",)),
        name="norm_cast",
    )(x, g.reshape(1, d))


def _resid_body(x_ref, f_ref, g_ref, gn_ref, xo_ref, no_ref, *, norm_next):
    xn = x_ref[...] + _rms(f_ref[...], g_ref[...])
    xo_ref[...] = xn
    if norm_next:
        no_ref[...] = _rms(xn, gn_ref[...]).astype(no_ref.dtype)
    else:
        no_ref[...] = xn.astype(no_ref.dtype)


def _resid_norm(x, f, g, g_next, *, tm=256):
    m, d = x.shape
    tm = _tile(m, tm)
    norm_next = g_next is not None
    gn = g_next if norm_next else g
    row = pl.BlockSpec((tm, d), lambda i: (i, 0))
    vec = pl.BlockSpec((1, d), lambda i: (0, 0))
    return pl.pallas_call(
        functools.partial(_resid_body, norm_next=norm_next),
        grid=(m // tm,),
        in_specs=[row, row, vec, vec],
        out_specs=[row, row],
        out_shape=[jax.ShapeDtypeStruct((m, d), F32), jax.ShapeDtypeStruct((m, d), BF16)],
        compiler_params=_cparams(("parallel",)),
        name="resid_norm",
    )(x, f, g.reshape(1, d), gn.reshape(1, d))


def _head_allsum(p, heads):
    q = p + pltpu.roll(p, 2 * heads, axis=1)
    return q + pltpu.roll(q, heads, axis=1)


def _col_slices(x):
    return [x[:, i * LANES:(i + 1) * LANES] for i in range(x.shape[1] // LANES)]


def _split2(x):
    hi = x.astype(BF16)
    return hi, (x - hi.astype(F32)).astype(BF16)


def _group_select(heads):
    src = np.arange(LANES)[:, None]
    dst = np.arange(2 * LANES)[None, :]
    mats = [(src == (2 * p + dst // LANES) * heads + dst % heads) for p in range(2)]
    return jnp.asarray(np.stack(mats), BF16)


def _prep_body(fm_ref, fl_ref, pm_ref, plo_ref, spm_ref, spl_ref, mum_ref, mul_ref, w0_ref, w2_ref,
               a0_ref, a2_ref, g2_ref, kk_ref, ka_ref, rk_ref, sel_ref,
               e_nkk, e_d, e_b, e_k, e_r, v_out, bonus_out, g_out, *, seq_shift, heads):
    width = heads * RWKV_HEAD

    def shifted(x, p_ref, sp_ref):
        if not seq_shift:
            return p_ref[...]
        before = jnp.where(pl.program_id(0) == 0, sp_ref[...], p_ref[7:8, :])
        row = lax.broadcasted_iota(jnp.int32, x.shape, 0)
        return jnp.where(row == 0, before, pltpu.roll(x, 1, axis=0))

    x = fm_ref[...]
    xm = x + (shifted(x, pm_ref, spm_ref) - x) * mum_ref[...]
    lo = fl_ref[...]
    lm = lo + (shifted(lo, plo_ref, spl_ref) - lo) * mul_ref[...]

    r = xm[:, :width]
    k = xm[:, width:2 * width]
    v = xm[:, 2 * width:]

    u = w0_ref[...] + jnp.dot(jnp.tanh(lm[:, :DECAY_LORA]).astype(BF16), w2_ref[...],
                              preferred_element_type=F32)
    w_log = -(jnp.maximum(-u, 0.0) + jnp.log1p(jnp.exp(-jnp.abs(u)))) - 0.5
    decay = jnp.exp(-jnp.exp(w_log))
    a = jax.nn.sigmoid(a0_ref[...] + jnp.dot(lm[:, DECAY_LORA:2 * DECAY_LORA].astype(BF16),
                                             a2_ref[...], preferred_element_type=F32))
    g = jnp.dot(jax.nn.sigmoid(lm[:, DECAY_LORA:]).astype(BF16), g2_ref[...],
                preferred_element_type=F32)

    kk = k * kk_ref[...]
    sq = _col_slices(kk * kk)
    tot = sq[0]
    for s in sq[1:]:
        tot = tot + s
    den = jnp.maximum(jnp.sqrt(_head_allsum(tot, heads)), L2_EPS)
    kk = jnp.concatenate([c / den for c in _col_slices(kk)], axis=1)
    kf = k * (1.0 + (a - 1.0) * ka_ref[...])
    b = kk * a

    rk_parts = _col_slices(r * kf * rk_ref[...])
    rk_tot = rk_parts[0]
    for s in rk_parts[1:]:
        rk_tot = rk_tot + s
    rk_tot = _head_allsum(rk_tot, heads)
    v_out[...] = v
    bonus_out[...] = jnp.concatenate([rk_tot * c for c in _col_slices(v)], axis=1)
    g_out[...] = g

    rows = x.shape[0]

    def expand(val, ref):
        hi, lo = _split2(jnp.concatenate(_col_slices(val), axis=0))
        for p in range(2):
            sel = sel_ref[p]
            out = (jnp.dot(hi, sel, preferred_element_type=F32)
                   + jnp.dot(lo, sel, preferred_element_type=F32))
            for i in range(width // LANES):
                for q in range(2):
                    ref[4 * i + 2 * p + q] = out[i * rows:(i + 1) * rows, q * LANES:(q + 1) * LANES]

    expand(-kk, e_nkk)
    expand(decay, e_d)
    expand(b, e_b)
    expand(kf, e_k)
    expand(r, e_r)


def _rwkv_prep(feats, lora, shift_m, shift_l, prm, *, seq_shift, heads, tm):
    m = feats.shape[0]
    width = heads * RWKV_HEAD
    tm = _tile(m, tm)
    nt = m // tm
    if seq_shift:
        assert tm % 8 == 0
        prev_m, prev_l = feats, lora
        pm_spec = pl.BlockSpec((8, 3 * width), lambda i: (jnp.maximum(i * (tm // 8) - 1, 0), 0))
        pl_spec = pl.BlockSpec((8, LORA_PAD), lambda i: (jnp.maximum(i * (tm // 8) - 1, 0), 0))
    else:
        prev_m, prev_l = shift_m, shift_l
        pm_spec = pl.BlockSpec((tm, 3 * width), lambda i: (i, 0))
        pl_spec = pl.BlockSpec((tm, LORA_PAD), lambda i: (i, 0))

    def vec(n):
        return pl.BlockSpec((1, n), lambda i: (0, 0))

    def full(r_, c_):
        return pl.BlockSpec((r_, c_), lambda i: (0, 0))

    e_spec = pl.BlockSpec((RWKV_HEAD, tm, LANES), lambda i: (0, i, 0))
    c_spec = pl.BlockSpec((tm, width), lambda i: (i, 0))
    e_shape = jax.ShapeDtypeStruct((RWKV_HEAD, m, LANES), F32)
    c_shape = jax.ShapeDtypeStruct((m, width), F32)
    return pl.pallas_call(
        functools.partial(_prep_body, seq_shift=seq_shift, heads=heads),
        grid=(nt,),
        in_specs=[pl.BlockSpec((tm, 3 * width), lambda i: (i, 0)),
                  pl.BlockSpec((tm, LORA_PAD), lambda i: (i, 0)),
                  pm_spec, pl_spec, vec(3 * width), vec(LORA_PAD), vec(3 * width), vec(LORA_PAD),
                  vec(width), full(DECAY_LORA, width), vec(width), full(DECAY_LORA, width),
                  full(LORA_PAD - DECAY_LORA, width), vec(width), vec(width), vec(width),
                  pl.BlockSpec((2, LANES, 2 * LANES), lambda i: (0, 0, 0))],
        out_specs=[e_spec] * 5 + [c_spec] * 3,
        out_shape=[e_shape] * 5 + [c_shape] * 3,
        compiler_params=_cparams(("parallel",)),
        name="rwkv_prep",
    )(feats, lora, prev_m, prev_l, shift_m[:1], shift_l[:1], prm["mu_m"], prm["mu_l"], prm["w0"],
      prm["w2"], prm["a0"], prm["a2"], prm["g2"], prm["k_k"], prm["k_a"], prm["r_k"],
      _group_select(heads))


N_ACC = 4


def _tree_sum(parts):
    while len(parts) > 1:
        parts = [parts[i] + parts[i + 1] for i in range(0, len(parts), 2)]
    return parts[0]


def _wkv_sa(s_ref, nkk_ref, t):
    acc = [None] * N_ACC
    for kx in range(RWKV_HEAD):
        term = s_ref[kx] * nkk_ref[kx, pl.ds(t, 1), :]
        acc[kx % N_ACC] = term if acc[kx % N_ACC] is None else acc[kx % N_ACC] + term
    return _tree_sum(acc)


def _wkv_step(s_ref, nkk_ref, d_ref, b_ref, k_ref, r_ref, v_t, sa, t, t_next):
    yacc = [None] * N_ACC
    sacc = [None] * N_ACC
    for kx in range(RWKV_HEAD):
        s_new = (s_ref[kx] * d_ref[kx, pl.ds(t, 1), :] + sa * b_ref[kx, pl.ds(t, 1), :]
                 + v_t * k_ref[kx, pl.ds(t, 1), :])
        s_ref[kx] = s_new
        term = s_new * r_ref[kx, pl.ds(t, 1), :]
        yacc[kx % N_ACC] = term if yacc[kx % N_ACC] is None else yacc[kx % N_ACC] + term
        if t_next is not None:
            term = s_new * nkk_ref[kx, pl.ds(t_next, 1), :]
            sacc[kx % N_ACC] = term if sacc[kx % N_ACC] is None else sacc[kx % N_ACC] + term
    return _tree_sum(yacc), (None if t_next is None else _tree_sum(sacc))


def _scan_seq_body(s0_ref, nkk_ref, d_ref, b_ref, k_ref, r_ref, v_ref, y_ref, sout_ref, s_ref, *,
                   tc):
    c = pl.program_id(0)

    @pl.when(c == 0)
    def _():
        s_ref[...] = s0_ref[...]

    def step(t, sa):
        y_ref[t], sa_next = _wkv_step(s_ref, nkk_ref, d_ref, b_ref, k_ref, r_ref, v_ref[t], sa, t,
                                      jnp.minimum(t + 1, tc - 1))
        return sa_next

    lax.fori_loop(0, tc, step, _wkv_sa(s_ref, nkk_ref, 0), unroll=4)

    @pl.when(c == pl.num_programs(0) - 1)
    def _():
        sout_ref[...] = s_ref[...]


def _wkv_scan_seq(s0, e_list, v, *, tc=64):
    t_len = v.shape[0]
    tc = _tile(t_len, tc)
    s_spec = pl.BlockSpec(s0.shape, lambda c: (0, 0, 0))
    e_spec = pl.BlockSpec((RWKV_HEAD, tc, LANES), lambda c: (0, c, 0))
    v_spec = pl.BlockSpec((tc,) + v.shape[1:], lambda c: (c, 0, 0))
    return pl.pallas_call(
        functools.partial(_scan_seq_body, tc=tc),
        grid=(t_len // tc,),
        in_specs=[s_spec] + [e_spec] * 5 + [v_spec],
        out_specs=[v_spec, s_spec],
        out_shape=[jax.ShapeDtypeStruct(v.shape, F32), jax.ShapeDtypeStruct(s0.shape, F32)],
        scratch_shapes=[pltpu.VMEM(s0.shape, F32)],
        compiler_params=_cparams(("arbitrary",)),
        name="wkv_scan",
    )(s0, *e_list, v)


def _scan_batch_body(s0_ref, nkk_ref, d_ref, b_ref, k_ref, r_ref, v_ref, y_ref, sout_ref):
    bidx = pl.program_id(0)
    s = sout_ref.at[0]
    s[...] = s0_ref[0]
    y_ref[0], _ = _wkv_step(s, nkk_ref, d_ref, b_ref, k_ref, r_ref, v_ref[0],
                            _wkv_sa(s, nkk_ref, bidx), bidx, None)


def _wkv_scan_batch(s0, e_list, v):
    nb = v.shape[0]
    s_spec = pl.BlockSpec((1,) + s0.shape[1:], lambda b: (b, 0, 0, 0))
    e_spec = pl.BlockSpec((RWKV_HEAD, nb, LANES), lambda b: (0, 0, 0))
    v_spec = pl.BlockSpec((1,) + v.shape[1:], lambda b: (b, 0, 0))
    return pl.pallas_call(
        _scan_batch_body,
        grid=(nb,),
        in_specs=[s_spec] + [e_spec] * 5 + [v_spec],
        out_specs=[v_spec, s_spec],
        out_shape=[jax.ShapeDtypeStruct(v.shape, F32), jax.ShapeDtypeStruct(s0.shape, F32)],
        compiler_params=_cparams(("arbitrary",)),
        name="wkv_step",
    )(s0, *e_list, v)


def _post_body(y_ref, bonus_ref, g_ref, lnw_ref, lnb_ref, o_ref, *, heads):
    inv_n = 1.0 / RWKV_HEAD

    def head_sum(x):
        cols = _col_slices(x)
        tot = cols[0]
        for c in cols[1:]:
            tot = tot + c
        return _head_allsum(tot, heads)

    def tiled(stat, like):
        return jnp.concatenate([stat] * (like.shape[1] // LANES), axis=1)

    y = y_ref[...]
    yc = y - tiled(head_sum(y) * inv_n, y)
    var = head_sum(yc * yc) * inv_n
    yn = yc * tiled(lax.rsqrt(var + GN_EPS), y) * lnw_ref[...] + lnb_ref[...]
    o_ref[...] = ((yn + bonus_ref[...]) * g_ref[...]).astype(o_ref.dtype)


def _rwkv_post(y, bonus, g, prm, *, heads, tm=256):
    m, width = y.shape
    tm = _tile(m, tm)
    row = pl.BlockSpec((tm, width), lambda i: (i, 0))
    vec = pl.BlockSpec((1, width), lambda i: (0, 0))
    return pl.pallas_call(
        functools.partial(_post_body, heads=heads),
        grid=(m // tm,),
        in_specs=[row] * 3 + [vec] * 2,
        out_specs=row,
        out_shape=jax.ShapeDtypeStruct((m, width), BF16),
        compiler_params=_cparams(("parallel",)),
        name="rwkv_post",
    )(y, bonus, g, prm["ln_w"], prm["ln_b"])


SB_TILE = 512
SB_SUB = 256
SB_HEADS_PER_STEP = 4
LOG2E = 1.4426950408889634


def _log_sigmoid_pair(z):
    lsz = jnp.minimum(z, 0.0) - jnp.log(1.0 + jnp.exp2(jnp.abs(z) * (-LOG2E)))
    return lsz, lsz - z


def _sb_weights(lsz, lsn, n_blocks, tri2_ref, carry, valid):
    rows = lsz.shape[0] // n_blocks
    hi = lsn.astype(BF16)
    lo = (lsn - hi.astype(F32)).astype(BF16)
    if n_blocks == 1:
        tri = tri2_ref[:lsz.shape[1]]
        logw = lsz + (jnp.dot(hi, tri, preferred_element_type=F32)
                      + jnp.dot(lo, tri, preferred_element_type=F32))
    else:
        logw = lsz + jnp.dot(jnp.concatenate([hi, lo], axis=1), tri2_ref[...],
                             preferred_element_type=F32)
    taken = jnp.sum(lsn, axis=1, keepdims=True)
    weights = []
    for b in range(n_blocks):
        sl = slice(b * rows, (b + 1) * rows)
        w = jnp.exp(logw[sl] + carry)
        if valid is not None:
            w = jnp.where(valid[sl], w, 0.0)
        weights.append(w.astype(BF16))
        carry = carry + taken[sl]
    return weights, carry


def _tri2(n):
    tri = (lax.broadcasted_iota(jnp.int32, (n, n), 0)
           > lax.broadcasted_iota(jnp.int32, (n, n), 1)).astype(BF16)
    return jnp.concatenate([tri, tri], axis=0)


def _suffix_sum(ls, tri_ref):
    hi = ls.astype(BF16)
    lo = (ls - hi.astype(F32)).astype(BF16)
    tri = tri_ref[...]
    return (jnp.dot(hi, tri, preferred_element_type=F32)
            + jnp.dot(lo, tri, preferred_element_type=F32))


def _sbp_body(qi_ref, kb_ref, bias_ref, q_ref, k_ref, v_ref, tri_ref, g_ref,
              o_ref, qs_ref, acc_ref, carry_ref):
    hp = pl.program_id(0)
    s = pl.program_id(1)
    tile = q_ref.shape[0]
    n_sub = tri_ref.shape[0]
    n_heads = q_ref.shape[1] // SB_HEAD
    diagonal = kb_ref[s] == qi_ref[s]

    def visit(masked):
        if masked:
            causal = (lax.broadcasted_iota(jnp.int32, (tile, tile), 1)
                      < lax.broadcasted_iota(jnp.int32, (tile, tile), 0))
        for hh in range(n_heads):
            cols = slice(hh * SB_HEAD, (hh + 1) * SB_HEAD)
            kmat = k_ref[:, cols].astype(BF16)
            vmat = v_ref[:, cols].astype(BF16)
            z = lax.dot_general(qs_ref[:, cols], kmat, (((1,), (1,)), ((), ())),
                                preferred_element_type=F32) + bias_ref[hp * n_heads + hh]
            lsz, lsn = _log_sigmoid_pair(z)
            if masked:
                lsn = jnp.where(causal, lsn, 0.0)
            carry = carry_ref[hh]
            w_parts = [None] * (tile // n_sub)
            for sub in reversed(range(tile // n_sub)):
                sl = slice(sub * n_sub, (sub + 1) * n_sub)
                w = jnp.exp(lsz[:, sl] + _suffix_sum(lsn[:, sl], tri_ref) + carry)
                if masked:
                    w = jnp.where(causal[:, sl], w, 0.0)
                w_parts[sub] = w.astype(BF16)
                carry = carry + jnp.sum(lsn[:, sl], axis=1, keepdims=True)
            carry_ref[hh] = carry
            acc_ref[:, cols] += jnp.dot(jnp.concatenate(w_parts, axis=1), vmat,
                                        preferred_element_type=F32)

    @pl.when(diagonal)
    def _():
        qs_ref[...] = (q_ref[...] * (SB_HEAD ** -0.5)).astype(BF16)
        acc_ref[...] = jnp.zeros_like(acc_ref)
        carry_ref[...] = jnp.zeros_like(carry_ref)
        visit(True)

    @pl.when(jnp.logical_not(diagonal))
    def _():
        visit(False)

    @pl.when(kb_ref[s] == 0)
    def _():
        for hh in range(n_heads):
            cols = slice(hh * SB_HEAD, (hh + 1) * SB_HEAD)
            o_ref[:, cols] = _rms(acc_ref[:, cols], g_ref[hh]).astype(o_ref.dtype)


def _sb_schedule(n_tiles):
    qi = [i for i in range(n_tiles) for _ in range(i + 1)]
    kb = [j for i in range(n_tiles) for j in range(i, -1, -1)]
    return [jnp.asarray(np.asarray(a, np.int32)) for a in (qi, kb)]


def _sb_prompt(qkv, q_col, bias, gain, *, heads):
    t_len = qkv.shape[0]
    tile = _tile(t_len, SB_TILE)
    sched = _sb_schedule(t_len // tile)
    nsteps = sched[0].shape[0]
    sub = min(SB_SUB, tile)
    hps = SB_HEADS_PER_STEP
    assert heads % hps == 0 and q_col % hps == 0
    wide = hps * SB_HEAD
    qc = q_col // hps
    grid_spec = pltpu.PrefetchScalarGridSpec(
        num_scalar_prefetch=3,
        grid=(heads // hps, nsteps),
        in_specs=[
            pl.BlockSpec((tile, wide), lambda h, s, qi, kb, b: (qi[s], qc + h)),
            pl.BlockSpec((tile, wide), lambda h, s, qi, kb, b: (kb[s], qc + heads // hps + h)),
            pl.BlockSpec((tile, wide), lambda h, s, qi, kb, b: (kb[s], qc + 2 * (heads // hps) + h)),
            pl.BlockSpec((sub, sub), lambda h, s, qi, kb, b: (0, 0)),
            pl.BlockSpec((hps, 1, SB_HEAD), lambda h, s, qi, kb, b: (h, 0, 0)),
        ],
        out_specs=pl.BlockSpec((tile, wide), lambda h, s, qi, kb, b: (qi[s], h)),
        scratch_shapes=[pltpu.VMEM((tile, wide), BF16), pltpu.VMEM((tile, wide), F32),
                        pltpu.VMEM((hps, tile, 1), F32)],
    )
    return pl.pallas_call(
        _sbp_body,
        grid_spec=grid_spec,
        out_shape=jax.ShapeDtypeStruct((t_len, heads * SB_HEAD), BF16),
        compiler_params=_cparams(("parallel", "arbitrary")),
        name="sb_prompt",
    )(*sched, bias, qkv, qkv, qkv, _tri2(sub)[:sub], gain.reshape(heads, 1, SB_HEAD))


SBS_PAGES = 4


def _sbs_body(pt_ref, q_ref, *refs, heads, pages):
    k_refs, v_refs = refs[:pages], refs[pages:2 * pages]
    tri_ref, bias_ref, g_ref, o_ref, qs_ref, acc_ref, carry_ref = refs[2 * pages:]
    p = pl.program_id(1)
    rows = k_refs[0].shape[1] * heads
    n_sub = tri_ref.shape[1]
    subs = list(reversed(range(rows // n_sub)))

    @pl.when(p == 0)
    def _():
        qs_ref[...] = (q_ref[0] * (SB_HEAD ** -0.5)).astype(BF16)
        acc_ref[...] = jnp.zeros_like(acc_ref)
        carry_ref[...] = jnp.zeros_like(carry_ref)

    z2_blocks = []
    for k_ref in k_refs:
        kmat = k_ref[0].reshape(rows, SB_HEAD).astype(BF16)
        z2 = lax.dot_general(qs_ref[...], kmat, (((1,), (1,)), ((), ())),
                             preferred_element_type=F32) + bias_ref[...]
        z2_blocks += [z2[:, u * n_sub:(u + 1) * n_sub] for u in subs]
    stacked = (len(z2_blocks) * heads, n_sub)
    own_head = (lax.broadcasted_iota(jnp.int32, stacked, 1) % heads
                == lax.broadcasted_iota(jnp.int32, stacked, 0) % heads)
    lsz, lsn = _log_sigmoid_pair(jnp.concatenate(z2_blocks, axis=0))
    weights, carry_ref[...] = _sb_weights(lsz, jnp.where(own_head, lsn, 0.0), len(z2_blocks),
                                          tri_ref, carry_ref[...], own_head)
    acc = acc_ref[...]
    for u, v_ref in enumerate(v_refs):
        vmat = v_ref[0].reshape(rows, SB_HEAD).astype(BF16)
        w_page = weights[u * len(subs):(u + 1) * len(subs)][::-1]
        acc = acc + jnp.dot(jnp.concatenate(w_page, axis=1), vmat, preferred_element_type=F32)
    acc_ref[...] = acc

    @pl.when(p == pl.num_programs(1) - 1)
    def _():
        o_ref[0] = _rms(acc_ref[...], g_ref[...]).astype(o_ref.dtype)


def _sb_sample(q, cache_k, cache_v, page_table, bias, gain, *, heads):
    nb = q.shape[0]
    n_pages = page_table.shape[1]
    page = cache_k.shape[1]
    sub = min(SB_SUB, page * heads)
    pages = SBS_PAGES
    while n_pages % pages:
        pages //= 2

    def kv_spec(u):
        return pl.BlockSpec((1, page, heads, SB_HEAD),
                            lambda b, p, pt: (pt[b, n_pages - 1 - (p * pages + u)], 0, 0, 0))

    kv_specs = [kv_spec(u) for u in range(pages)]
    grid_spec = pltpu.PrefetchScalarGridSpec(
        num_scalar_prefetch=1,
        grid=(nb, n_pages // pages),
        in_specs=[pl.BlockSpec((1, heads, SB_HEAD), lambda b, p, pt: (b, 0, 0))]
                 + kv_specs + kv_specs
                 + [pl.BlockSpec((2 * sub, sub), lambda b, p, pt: (0, 0)),
                    pl.BlockSpec((heads, 1), lambda b, p, pt: (0, 0)),
                    pl.BlockSpec((heads, SB_HEAD), lambda b, p, pt: (0, 0))],
        out_specs=pl.BlockSpec((1, heads, SB_HEAD), lambda b, p, pt: (b, 0, 0)),
        scratch_shapes=[pltpu.VMEM((heads, SB_HEAD), BF16), pltpu.VMEM((heads, SB_HEAD), F32),
                        pltpu.VMEM((heads, 1), F32)],
    )
    return pl.pallas_call(
        functools.partial(_sbs_body, heads=heads, pages=pages),
        grid_spec=grid_spec,
        out_shape=jax.ShapeDtypeStruct((nb, heads, SB_HEAD), BF16),
        compiler_params=_cparams(("parallel", "arbitrary")),
        name="sb_sample",
    )(page_table, q, *([cache_k] * pages), *([cache_v] * pages), _tri2(sub),
      bias.reshape(heads, 1), gain)


def _perm_cols(w, heads):
    lead = w.shape[:-1]
    return w.reshape(lead + (heads, RWKV_HEAD)).swapaxes(-1, -2).reshape(lead + (heads * RWKV_HEAD,))


def _unperm_cols(w, heads):
    lead = w.shape[:-1]
    return w.reshape(lead + (RWKV_HEAD, heads)).swapaxes(-1, -2).reshape(lead + (heads * RWKV_HEAD,))


def _state_in(s):
    heads = s.shape[-3]
    lead = s.shape[:-3]
    s = jnp.moveaxis(s, -3, -1)
    s = jnp.swapaxes(s, -3, -2)
    return s.reshape(lead + (RWKV_HEAD, RWKV_HEAD * heads // LANES, LANES))


def _state_out(s, heads):
    lead = s.shape[:-3]
    s = s.reshape(lead + (RWKV_HEAD, RWKV_HEAD, heads))
    s = jnp.swapaxes(s, -3, -2)
    return jnp.moveaxis(s, -1, -3)


def _layer(x, pe, shift_prev, wkv0, attend, wts, *, seq_shift):
    heads = wts["heads"]
    width = heads * RWKV_HEAD
    sb_heads = wts["sb_heads"]
    sb_width = sb_heads * SB_HEAD
    m = x.shape[0]

    n_lora = DECAY_LORA + AAA_LORA + GATE_LORA
    h = _norm_cast(x, wts["norm_mix_pre"])
    feats = _matmul(h, wts["w_rkv"], name="mm_in")
    qkv = _matmul(h, wts["w_qkv"], name="mm_qkv")
    lora = _matmul(h, wts["w_lora"], name="mm_lora")

    rows = slice(m - 1, m) if seq_shift else slice(0, m)
    shift_new = jnp.concatenate(
        [_unperm_cols(feats[rows].reshape(-1, 3, width), heads).reshape(-1, 3 * width),
         lora[rows, :n_lora]], axis=1)
    tm = _tile(m, 64) if seq_shift else m
    outs = _rwkv_prep(feats, lora, shift_prev[:, :3 * width], shift_prev[:, 3 * width:], wts,
                      seq_shift=seq_shift, heads=heads, tm=tm)
    e_list, (v_c, bonus_c, g_c) = outs[:5], outs[5:]
    v_rows = v_c.reshape(m, width // LANES, LANES)
    if seq_shift:
        y, s_fin = _wkv_scan_seq(_state_in(wkv0[0]), e_list, v_rows)
        wkv_new = _state_out(s_fin, heads)[None]
    else:
        y, s_fin = _wkv_scan_batch(_state_in(wkv0), e_list, v_rows)
        wkv_new = _state_out(s_fin, heads)
    r_out = _rwkv_post(y.reshape(m, width), bonus_c, g_c, wts, heads=heads)

    a_out = attend(qkv)
    kh = qkv[:, sb_width:2 * sb_width].reshape(m, sb_heads, SB_HEAD)
    vh = qkv[:, 2 * sb_width:].reshape(m, sb_heads, SB_HEAD)

    mix = _matmul_cat(r_out, a_out, wts["w_out_r"], wts["w_out_a"])
    x1, h2 = _resid_norm(x, mix, wts["norm_mix_post"], wts["norm_ffn_pre"])
    up = _matmul(h2, wts["w_up"], out_dtype=BF16, relu2=True, name="mm_up")
    f = _matmul(up, wts["w_down"], name="mm_down")
    x2, x2b = _resid_norm(x1, f, wts["norm_ffn_post"], None)
    out = _ple(x2b, wts["w_ple_gate"], x2, pe.astype(BF16), wts["w_ple_proj"])
    return out, shift_new, wkv_new, kh, vh


def _perm3(w, width, heads):
    lead = w.shape[:-1]
    return _perm_cols(w.reshape(lead + (3, width)), heads).reshape(lead + (3 * width,))


def _shift_in(s, width, heads):
    pad = LORA_PAD - (s.shape[1] - 3 * width)
    return jnp.concatenate([_perm3(s[:, :3 * width], width, heads), s[:, 3 * width:],
                            jnp.zeros((s.shape[0], pad), s.dtype)], axis=1)


def kernel(x_prompt, x_sample, p_prompt, p_sample, state_rwkv_shift, state_rwkv_wkv, cache_k, cache_v, page_table, norm_mix_pre, norm_mix_post, norm_ffn_pre, norm_ffn_post, w_in, rwkv_mu, rwkv_w0, rwkv_w2, rwkv_a0, rwkv_a2, rwkv_g2, rwkv_k_k, rwkv_k_a, rwkv_r_k, rwkv_ln_w, rwkv_ln_b, sb_norm, sb_bias, w_out, w_up, w_down, w_ple_gate, w_ple_proj):
    depth = w_in.shape[0]
    heads = rwkv_r_k.shape[1]
    width = heads * RWKV_HEAD
    sb_heads = sb_bias.shape[1]
    sb_width = sb_heads * SB_HEAD
    nb, t_len, d_model = x_prompt.shape
    db = x_sample.shape[0]
    assert nb == 1 and x_sample.shape[1] == 1 and LANES == 4 * heads
    rwkv_proj = rwkv_mu.shape[1]
    n_lora = rwkv_proj - 3 * width
    assert n_lora == DECAY_LORA + AAA_LORA + GATE_LORA and sb_width + width == w_out.shape[1]

    yp, ys = x_prompt[0], x_sample[:, 0]
    res = [[] for _ in range(8)]
    for i in range(depth):
        wi = w_in[i]
        lpad = jnp.zeros((d_model, LORA_PAD - n_lora), wi.dtype)
        mu = rwkv_mu[i]
        a2p = jnp.concatenate([rwkv_a2[i], jnp.zeros((DECAY_LORA - AAA_LORA, width), F32)], axis=0)
        g2p = jnp.concatenate([jnp.zeros((AAA_LORA, width), F32), rwkv_g2[i],
                               jnp.zeros((LORA_PAD - n_lora, width), F32)], axis=0)
        wo = w_out[i]
        wts = dict(
            heads=heads, sb_heads=sb_heads,
            norm_mix_pre=norm_mix_pre[i], norm_mix_post=norm_mix_post[i],
            norm_ffn_pre=norm_ffn_pre[i], norm_ffn_post=norm_ffn_post[i],
            w_rkv=_perm3(wi[:, :3 * width].astype(BF16), width, heads),
            w_qkv=wi[:, rwkv_proj:].astype(BF16),
            w_lora=jnp.concatenate([wi[:, 3 * width:rwkv_proj], lpad], axis=1).astype(BF16),
            mu_m=_perm3(mu[:3 * width], width, heads).reshape(1, -1),
            mu_l=jnp.concatenate([mu[3 * width:], jnp.zeros((LORA_PAD - n_lora,), F32)]).reshape(1, -1),
            w0=_perm_cols(rwkv_w0[i], heads).reshape(1, -1),
            w2=_perm_cols(rwkv_w2[i], heads).astype(BF16),
            a0=_perm_cols(rwkv_a0[i], heads).reshape(1, -1),
            a2=_perm_cols(a2p, heads).astype(BF16),
            g2=_perm_cols(g2p, heads).astype(BF16),
            k_k=_perm_cols(rwkv_k_k[i], heads).reshape(1, -1),
            k_a=_perm_cols(rwkv_k_a[i], heads).reshape(1, -1),
            r_k=_perm_cols(rwkv_r_k[i].reshape(-1), heads).reshape(1, -1),
            ln_w=_perm_cols(rwkv_ln_w[i], heads).reshape(1, -1),
            ln_b=_perm_cols(rwkv_ln_b[i], heads).reshape(1, -1),
            w_out_r=wo[:width].astype(BF16).reshape(heads, RWKV_HEAD, -1).swapaxes(0, 1)
                    .reshape(width, -1),
            w_out_a=wo[width:].astype(BF16),
            w_up=w_up[i].astype(BF16), w_down=w_down[i].astype(BF16),
            w_ple_gate=w_ple_gate[i].astype(BF16), w_ple_proj=w_ple_proj[i].astype(BF16),
        )
        bias, gain = sb_bias[i], sb_norm[i]

        attend_p = lambda qkv: _sb_prompt(qkv, 0, bias, gain, heads=sb_heads)
        zero_shift = _shift_in(jnp.zeros((1, rwkv_proj), F32), width, heads)
        zero_wkv = jnp.zeros((1, heads, RWKV_HEAD, RWKV_HEAD), F32)
        yp, sp, wp, kp, vp = _layer(yp, p_prompt[i, 0], zero_shift, zero_wkv, attend_p, wts,
                                    seq_shift=True)

        n_phys = cache_k.shape[1]
        ck = cache_k.reshape((depth * n_phys,) + cache_k.shape[2:])
        cv = cache_v.reshape((depth * n_phys,) + cache_v.shape[2:])
        pages_i = page_table + i * n_phys
        attend_s = lambda qkv: _sb_sample(
            qkv[:, :sb_width].reshape(db, sb_heads, SB_HEAD), ck, cv, pages_i, bias, gain,
            heads=sb_heads).reshape(db, sb_width)
        ys, ss, ws, kn, vn = _layer(ys, p_sample[i, :, 0], _shift_in(state_rwkv_shift[i], width, heads),
                                    state_rwkv_wkv[i], attend_s, wts, seq_shift=False)
        for lst, val in zip(res, (sp, ss, wp, ws, kp[None], vp[None], kn[:, None], vn[:, None])):
            lst.append(val)
    return (yp[None], ys[:, None]) + tuple(jnp.stack(r) for r in res)
```

```python
import functools

import jax
import jax.numpy as jnp
import numpy as np
from jax import lax
from jax.experimental import pallas as pl
from jax.experimental.pallas import tpu as pltpu

F32 = jnp.float32
BF16 = jnp.bfloat16

RMS_EPS = 1e-6
GN_EPS = 64e-5
L2_EPS = 1e-12

LANES = 128
RWKV_HEAD = 64
SB_HEAD = 128
DECAY_LORA = 128
AAA_LORA = 96
GATE_LORA = 256
LORA_PAD = 512
VMEM_LIMIT = 52 * 1024 * 1024


def _cparams(sem):
    return pltpu.CompilerParams(dimension_semantics=sem, vmem_limit_bytes=VMEM_LIMIT)


def _tile(n, pref):
    if n <= pref:
        return n
    t = pref
    while n % t:
        t //= 2
    return t


def _mm_body(a_ref, b_ref, o_ref, acc_ref, *, nk, relu2):
    def finish(acc):
        if relu2:
            acc = jnp.square(jnp.maximum(acc, 0.0))
        o_ref[...] = acc.astype(o_ref.dtype)

    if nk == 1:
        finish(jnp.dot(a_ref[...], b_ref[...], preferred_element_type=F32))
    else:
        k = pl.program_id(2)

        @pl.when(k == 0)
        def _():
            acc_ref[...] = jnp.zeros_like(acc_ref)

        acc_ref[...] += jnp.dot(a_ref[...], b_ref[...], preferred_element_type=F32)

        @pl.when(k == nk - 1)
        def _():
            finish(acc_ref[...])


def _matmul(a, b, *, out_dtype=F32, relu2=False, tm=1024, tn=1024, tk=2048, name="mm"):
    m, kdim = a.shape
    n = b.shape[1]
    tm, tn = _tile(m, tm), _tile(n, tn)
    tk = kdim if kdim <= 4096 else _tile(kdim, tk)
    nk = kdim // tk
    return pl.pallas_call(
        functools.partial(_mm_body, nk=nk, relu2=relu2),
        grid=(n // tn, m // tm, nk),
        in_specs=[pl.BlockSpec((tm, tk), lambda j, i, k: (i, k)),
                  pl.BlockSpec((tk, tn), lambda j, i, k: (k, j))],
        out_specs=pl.BlockSpec((tm, tn), lambda j, i, k: (i, j)),
        out_shape=jax.ShapeDtypeStruct((m, n), out_dtype),
        scratch_shapes=[pltpu.VMEM((tm, tn) if nk > 1 else (8, LANES), F32)],
        compiler_params=_cparams(("parallel", "parallel", "arbitrary")),
        name=name,
    )(a, b)


def _mm2_body(a1_ref, a2_ref, b1_ref, b2_ref, o_ref):
    o_ref[...] = (jnp.dot(a1_ref[...], b1_ref[...], preferred_element_type=F32)
                  + jnp.dot(a2_ref[...], b2_ref[...], preferred_element_type=F32))


def _matmul_cat(a1, a2, b1, b2, *, tm=1024, tn=1024):
    m, k1 = a1.shape
    k2 = a2.shape[1]
    n = b1.shape[1]
    tm, tn = _tile(m, tm), _tile(n, tn)
    return pl.pallas_call(
        _mm2_body,
        grid=(n // tn, m // tm),
        in_specs=[pl.BlockSpec((tm, k1), lambda j, i: (i, 0)),
                  pl.BlockSpec((tm, k2), lambda j, i: (i, 0)),
                  pl.BlockSpec((k1, tn), lambda j, i: (0, j)),
                  pl.BlockSpec((k2, tn), lambda j, i: (0, j))],
        out_specs=pl.BlockSpec((tm, tn), lambda j, i: (i, j)),
        out_shape=jax.ShapeDtypeStruct((m, n), F32),
        compiler_params=_cparams(("parallel", "parallel")),
        name="mm_out",
    )(a1, a2, b1, b2)


def _ple_body(a_ref, b_ref, x_ref, pe_ref, wp_ref, o_ref):
    gate = jnp.dot(a_ref[...], b_ref[...], preferred_element_type=F32)
    proj = jnp.dot(pe_ref[...], wp_ref[...], preferred_element_type=F32)
    o_ref[...] = x_ref[...] + jax.nn.sigmoid(gate) * proj


def _ple(xb, wg, x, pe, wp, *, tm=512, tn=1024):
    m, kdim = xb.shape
    n = wg.shape[1]
    kp = pe.shape[1]
    tm, tn = _tile(m, tm), _tile(n, tn)
    return pl.pallas_call(
        _ple_body,
        grid=(n // tn, m // tm),
        in_specs=[pl.BlockSpec((tm, kdim), lambda j, i: (i, 0)),
                  pl.BlockSpec((kdim, tn), lambda j, i: (0, j)),
                  pl.BlockSpec((tm, tn), lambda j, i: (i, j)),
                  pl.BlockSpec((tm, kp), lambda j, i: (i, 0)),
                  pl.BlockSpec((kp, tn), lambda j, i: (0, j))],
        out_specs=pl.BlockSpec((tm, tn), lambda j, i: (i, j)),
        out_shape=jax.ShapeDtypeStruct((m, n), F32),
        compiler_params=_cparams(("parallel", "parallel")),
        name="ple",
    )(xb, wg, x, pe, wp)


def _rms(x, g):
    return x * lax.rsqrt(jnp.mean(x * x, axis=-1, keepdims=True) + RMS_EPS) * g


def _norm_cast_body(x_ref, g_ref, o_ref):
    o_ref[...] = _rms(x_ref[...], g_ref[...]).astype(o_ref.dtype)


def _norm_cast(x, g, *, tm=256):
    m, d = x.shape
    tm = _tile(m, tm)
    return pl.pallas_call(
        _norm_cast_body,
        grid=(m // tm,),
        in_specs=[pl.BlockSpec((tm, d), lambda i: (i, 0)),
                  pl.BlockSpec((1, d), lambda i: (0, 0))],
        out_specs=pl.BlockSpec((tm, d), lambda i: (i, 0)),
        out_shape=jax.ShapeDtypeStruct((m, d), BF16),
        compiler_params=_cparams(("parallel",)),
        name="norm_cast",
    )(x, g.reshape(1, d))


def _resid_body(x_ref, f_ref, g_ref, gn_ref, xo_ref, no_ref, *, norm_next):
    xn = x_ref[...] + _rms(f_ref[...], g_ref[...])
    xo_ref[...] = xn
    if norm_next:
        no_ref[...] = _rms(xn, gn_ref[...]).astype(no_ref.dtype)
    else:
        no_ref[...] = xn.astype(no_ref.dtype)


def _resid_norm(x, f, g, g_next, *, tm=256):
    m, d = x.shape
    tm = _tile(m, tm)
    norm_next = g_next is not None
    gn = g_next if norm_next else g
    row = pl.BlockSpec((tm, d), lambda i: (i, 0))
    vec = pl.BlockSpec((1, d), lambda i: (0, 0))
    return pl.pallas_call(
        functools.partial(_resid_body, norm_next=norm_next),
        grid=(m // tm,),
        in_specs=[row, row, vec, vec],
        out_specs=[row, row],
        out_shape=[jax.ShapeDtypeStruct((m, d), F32), jax.ShapeDtypeStruct((m, d), BF16)],
        compiler_params=_cparams(("parallel",)),
        name="resid_norm",
    )(x, f, g.reshape(1, d), gn.reshape(1, d))


def _head_allsum(p, heads):
    q = p + pltpu.roll(p, 2 * heads, axis=1)
    return q + pltpu.roll(q, heads, axis=1)


def _col_slices(x):
    return [x[:, i * LANES:(i + 1) * LANES] for i in range(x.shape[1] // LANES)]


def _split2(x):
    hi = x.astype(BF16)
    return hi, (x - hi.astype(F32)).astype(BF16)


def _group_select(heads):
    src = np.arange(LANES)[:, None]
    dst = np.arange(2 * LANES)[None, :]
    mats = [(src == (2 * p + dst // LANES) * heads + dst % heads) for p in range(2)]
    return jnp.asarray(np.stack(mats), BF16)


def _prep_body(fm_ref, fl_ref, pm_ref, plo_ref, spm_ref, spl_ref, mum_ref, mul_ref, w0_ref, w2_ref,
               a0_ref, a2_ref, g2_ref, kk_ref, ka_ref, rk_ref, sel_ref,
               e_nkk, e_d, e_b, e_k, e_r, v_out, bonus_out, g_out, *, seq_shift, heads):
    width = heads * RWKV_HEAD

    def shifted(x, p_ref, sp_ref):
        if not seq_shift:
            return p_ref[...]
        before = jnp.where(pl.program_id(0) == 0, sp_ref[...], p_ref[7:8, :])
        row = lax.broadcasted_iota(jnp.int32, x.shape, 0)
        return jnp.where(row == 0, before, pltpu.roll(x, 1, axis=0))

    x = fm_ref[...]
    xm = x + (shifted(x, pm_ref, spm_ref) - x) * mum_ref[...]
    lo = fl_ref[...]
    lm = lo + (shifted(lo, plo_ref, spl_ref) - lo) * mul_ref[...]

    r = xm[:, :width]
    k = xm[:, width:2 * width]
    v = xm[:, 2 * width:]

    u = w0_ref[...] + jnp.dot(jnp.tanh(lm[:, :DECAY_LORA]).astype(BF16), w2_ref[...],
                              preferred_element_type=F32)
    w_log = -(jnp.maximum(-u, 0.0) + jnp.log1p(jnp.exp(-jnp.abs(u)))) - 0.5
    decay = jnp.exp(-jnp.exp(w_log))
    a = jax.nn.sigmoid(a0_ref[...] + jnp.dot(lm[:, DECAY_LORA:2 * DECAY_LORA].astype(BF16),
                                             a2_ref[...], preferred_element_type=F32))
    g = jnp.dot(jax.nn.sigmoid(lm[:, DECAY_LORA:]).astype(BF16), g2_ref[...],
                preferred_element_type=F32)

    kk = k * kk_ref[...]
    sq = _col_slices(kk * kk)
    tot = sq[0]
    for s in sq[1:]:
        tot = tot + s
    den = jnp.maximum(jnp.sqrt(_head_allsum(tot, heads)), L2_EPS)
    kk = jnp.concatenate([c / den for c in _col_slices(kk)], axis=1)
    kf = k * (1.0 + (a - 1.0) * ka_ref[...])
    b = kk * a

    rk_parts = _col_slices(r * kf * rk_ref[...])
    rk_tot = rk_parts[0]
    for s in rk_parts[1:]:
        rk_tot = rk_tot + s
    rk_tot = _head_allsum(rk_tot, heads)
    v_out[...] = v
    bonus_out[...] = jnp.concatenate([rk_tot * c for c in _col_slices(v)], axis=1)
    g_out[...] = g

    rows = x.shape[0]

    def expand(val, ref):
        hi, lo = _split2(jnp.concatenate(_col_slices(val), axis=0))
        for p in range(2):
            sel = sel_ref[p]
            out = (jnp.dot(hi, sel, preferred_element_type=F32)
                   + jnp.dot(lo, sel, preferred_element_type=F32))
            for i in range(width // LANES):
                for q in range(2):
                    ref[4 * i + 2 * p + q] = out[i * rows:(i + 1) * rows, q * LANES:(q + 1) * LANES]

    expand(-kk, e_nkk)
    expand(decay, e_d)
    expand(b, e_b)
    expand(kf, e_k)
    expand(r, e_r)


def _rwkv_prep(feats, lora, shift_m, shift_l, prm, *, seq_shift, heads, tm):
    m = feats.shape[0]
    width = heads * RWKV_HEAD
    tm = _tile(m, tm)
    nt = m // tm
    if seq_shift:
        assert tm % 8 == 0
        prev_m, prev_l = feats, lora
        pm_spec = pl.BlockSpec((8, 3 * width), lambda i: (jnp.maximum(i * (tm // 8) - 1, 0), 0))
        pl_spec = pl.BlockSpec((8, LORA_PAD), lambda i: (jnp.maximum(i * (tm // 8) - 1, 0), 0))
    else:
        prev_m, prev_l = shift_m, shift_l
        pm_spec = pl.BlockSpec((tm, 3 * width), lambda i: (i, 0))
        pl_spec = pl.BlockSpec((tm, LORA_PAD), lambda i: (i, 0))

    def vec(n):
        return pl.BlockSpec((1, n), lambda i: (0, 0))

    def full(r_, c_):
        return pl.BlockSpec((r_, c_), lambda i: (0, 0))

    e_spec = pl.BlockSpec((RWKV_HEAD, tm, LANES), lambda i: (0, i, 0))
    c_spec = pl.BlockSpec((tm, width), lambda i: (i, 0))
    e_shape = jax.ShapeDtypeStruct((RWKV_HEAD, m, LANES), F32)
    c_shape = jax.ShapeDtypeStruct((m, width), F32)
    return pl.pallas_call(
        functools.partial(_prep_body, seq_shift=seq_shift, heads=heads),
        grid=(nt,),
        in_specs=[pl.BlockSpec((tm, 3 * width), lambda i: (i, 0)),
                  pl.BlockSpec((tm, LORA_PAD), lambda i: (i, 0)),
                  pm_spec, pl_spec, vec(3 * width), vec(LORA_PAD), vec(3 * width), vec(LORA_PAD),
                  vec(width), full(DECAY_LORA, width), vec(width), full(DECAY_LORA, width),
                  full(LORA_PAD - DECAY_LORA, width), vec(width), vec(width), vec(width),
                  pl.BlockSpec((2, LANES, 2 * LANES), lambda i: (0, 0, 0))],
        out_specs=[e_spec] * 5 + [c_spec] * 3,
        out_shape=[e_shape] * 5 + [c_shape] * 3,
        compiler_params=_cparams(("parallel",)),
        name="rwkv_prep",
    )(feats, lora, prev_m, prev_l, shift_m[:1], shift_l[:1], prm["mu_m"], prm["mu_l"], prm["w0"],
      prm["w2"], prm["a0"], prm["a2"], prm["g2"], prm["k_k"], prm["k_a"], prm["r_k"],
      _group_select(heads))


N_ACC = 4


def _tree_sum(parts):
    while len(parts) > 1:
        parts = [parts[i] + parts[i + 1] for i in range(0, len(parts), 2)]
    return parts[0]


def _wkv_sa(s_ref, nkk_ref, t):
    acc = [None] * N_ACC
    for kx in range(RWKV_HEAD):
        term = s_ref[kx] * nkk_ref[kx, pl.ds(t, 1), :]
        acc[kx % N_ACC] = term if acc[kx % N_ACC] is None else acc[kx % N_ACC] + term
    return _tree_sum(acc)


def _wkv_step(s_ref, nkk_ref, d_ref, b_ref, k_ref, r_ref, v_t, sa, t, t_next):
    yacc = [None] * N_ACC
    sacc = [None] * N_ACC
    for kx in range(RWKV_HEAD):
        s_new = (s_ref[kx] * d_ref[kx, pl.ds(t, 1), :] + sa * b_ref[kx, pl.ds(t, 1), :]
                 + v_t * k_ref[kx, pl.ds(t, 1), :])
        s_ref[kx] = s_new
        term = s_new * r_ref[kx, pl.ds(t, 1), :]
        yacc[kx % N_ACC] = term if yacc[kx % N_ACC] is None else yacc[kx % N_ACC] + term
        if t_next is not None:
            term = s_new * nkk_ref[kx, pl.ds(t_next, 1), :]
            sacc[kx % N_ACC] = term if sacc[kx % N_ACC] is None else sacc[kx % N_ACC] + term
    return _tree_sum(yacc), (None if t_next is None else _tree_sum(sacc))


def _scan_seq_body(s0_ref, nkk_ref, d_ref, b_ref, k_ref, r_ref, v_ref, y_ref, sout_ref, s_ref, *,
                   tc):
    c = pl.program_id(0)

    @pl.when(c == 0)
    def _():
        s_ref[...] = s0_ref[...]

    def step(t, sa):
        y_ref[t], sa_next = _wkv_step(s_ref, nkk_ref, d_ref, b_ref, k_ref, r_ref, v_ref[t], sa, t,
                                      jnp.minimum(t + 1, tc - 1))
        return sa_next

    lax.fori_loop(0, tc, step, _wkv_sa(s_ref, nkk_ref, 0), unroll=8)

    @pl.when(c == pl.num_programs(0) - 1)
    def _():
        sout_ref[...] = s_ref[...]


def _wkv_scan_seq(s0, e_list, v, *, tc=64):
    t_len = v.shape[0]
    tc = _tile(t_len, tc)
    s_spec = pl.BlockSpec(s0.shape, lambda c: (0, 0, 0))
    e_spec = pl.BlockSpec((RWKV_HEAD, tc, LANES), lambda c: (0, c, 0))
    v_spec = pl.BlockSpec((tc,) + v.shape[1:], lambda c: (c, 0, 0))
    return pl.pallas_call(
        functools.partial(_scan_seq_body, tc=tc),
        grid=(t_len // tc,),
        in_specs=[s_spec] + [e_spec] * 5 + [v_spec],
        out_specs=[v_spec, s_spec],
        out_shape=[jax.ShapeDtypeStruct(v.shape, F32), jax.ShapeDtypeStruct(s0.shape, F32)],
        scratch_shapes=[pltpu.VMEM(s0.shape, F32)],
        compiler_params=_cparams(("arbitrary",)),
        name="wkv_scan",
    )(s0, *e_list, v)


def _scan_batch_body(s0_ref, nkk_ref, d_ref, b_ref, k_ref, r_ref, v_ref, y_ref, sout_ref):
    bidx = pl.program_id(0)
    s = sout_ref.at[0]
    s[...] = s0_ref[0]
    y_ref[0], _ = _wkv_step(s, nkk_ref, d_ref, b_ref, k_ref, r_ref, v_ref[0],
                            _wkv_sa(s, nkk_ref, bidx), bidx, None)


def _wkv_scan_batch(s0, e_list, v):
    nb = v.shape[0]
    s_spec = pl.BlockSpec((1,) + s0.shape[1:], lambda b: (b, 0, 0, 0))
    e_spec = pl.BlockSpec((RWKV_HEAD, nb, LANES), lambda b: (0, 0, 0))
    v_spec = pl.BlockSpec((1,) + v.shape[1:], lambda b: (b, 0, 0))
    return pl.pallas_call(
        _scan_batch_body,
        grid=(nb,),
        in_specs=[s_spec] + [e_spec] * 5 + [v_spec],
        out_specs=[v_spec, s_spec],
        out_shape=[jax.ShapeDtypeStruct(v.shape, F32), jax.ShapeDtypeStruct(s0.shape, F32)],
        compiler_params=_cparams(("arbitrary",)),
        name="wkv_step",
    )(s0, *e_list, v)


def _post_body(y_ref, bonus_ref, g_ref, lnw_ref, lnb_ref, o_ref, *, heads):
    inv_n = 1.0 / RWKV_HEAD

    def head_sum(x):
        cols = _col_slices(x)
        tot = cols[0]
        for c in cols[1:]:
            tot = tot + c
        return _head_allsum(tot, heads)

    def tiled(stat, like):
        return jnp.concatenate([stat] * (like.shape[1] // LANES), axis=1)

    y = y_ref[...]
    yc = y - tiled(head_sum(y) * inv_n, y)
    var = head_sum(yc * yc) * inv_n
    yn = yc * tiled(lax.rsqrt(var + GN_EPS), y) * lnw_ref[...] + lnb_ref[...]
    o_ref[...] = ((yn + bonus_ref[...]) * g_ref[...]).astype(o_ref.dtype)


def _rwkv_post(y, bonus, g, prm, *, heads, tm=256):
    m, width = y.shape
    tm = _tile(m, tm)
    row = pl.BlockSpec((tm, width), lambda i: (i, 0))
    vec = pl.BlockSpec((1, width), lambda i: (0, 0))
    return pl.pallas_call(
        functools.partial(_post_body, heads=heads),
        grid=(m // tm,),
        in_specs=[row] * 3 + [vec] * 2,
        out_specs=row,
        out_shape=jax.ShapeDtypeStruct((m, width), BF16),
        compiler_params=_cparams(("parallel",)),
        name="rwkv_post",
    )(y, bonus, g, prm["ln_w"], prm["ln_b"])


SB_TILE = 512
SB_SUB = 256
SB_HEADS_PER_STEP = 4
LOG2E = 1.4426950408889634


def _log_sigmoid_pair(z):
    lsz = jnp.minimum(z, 0.0) - jnp.log(1.0 + jnp.exp2(jnp.abs(z) * (-LOG2E)))
    return lsz, lsz - z


def _sb_weights(lsz, lsn, n_blocks, tri2_ref, carry, valid):
    rows = lsz.shape[0] // n_blocks
    hi = lsn.astype(BF16)
    lo = (lsn - hi.astype(F32)).astype(BF16)
    if n_blocks == 1:
        tri = tri2_ref[:lsz.shape[1]]
        logw = lsz + (jnp.dot(hi, tri, preferred_element_type=F32)
                      + jnp.dot(lo, tri, preferred_element_type=F32))
    else:
        logw = lsz + jnp.dot(jnp.concatenate([hi, lo], axis=1), tri2_ref[...],
                             preferred_element_type=F32)
    taken = jnp.sum(lsn, axis=1, keepdims=True)
    weights = []
    for b in range(n_blocks):
        sl = slice(b * rows, (b + 1) * rows)
        w = jnp.exp(logw[sl] + carry)
        if valid is not None:
            w = jnp.where(valid[sl], w, 0.0)
        weights.append(w.astype(BF16))
        carry = carry + taken[sl]
    return weights, carry


def _tri2(n):
    tri = (lax.broadcasted_iota(jnp.int32, (n, n), 0)
           > lax.broadcasted_iota(jnp.int32, (n, n), 1)).astype(BF16)
    return jnp.concatenate([tri, tri], axis=0)


def _suffix_sum(ls, tri_ref):
    hi = ls.astype(BF16)
    lo = (ls - hi.astype(F32)).astype(BF16)
    tri = tri_ref[...]
    return jnp.dot(jnp.concatenate([hi, lo], axis=1), jnp.concatenate([tri, tri], axis=0),
                   preferred_element_type=F32)


def _sbp_body(qi_ref, kb_ref, q_ref, k_ref, v_ref, tri_ref, g_ref, bias_ref, ones_ref,
              o_ref, qs_ref, acc_ref, carry_ref):
    s = pl.program_id(1)
    tile = q_ref.shape[0]
    n_sub = tri_ref.shape[0]
    n_heads = q_ref.shape[1] // SB_HEAD
    diagonal = kb_ref[s] == qi_ref[s]

    def visit(masked):
        if masked:
            causal = (lax.broadcasted_iota(jnp.int32, (tile, tile), 1)
                      < lax.broadcasted_iota(jnp.int32, (tile, tile), 0))
        for hh in range(n_heads):
            cols = slice(hh * SB_HEAD, (hh + 1) * SB_HEAD)
            kmat = jnp.concatenate([k_ref[:, cols].astype(BF16), ones_ref[...]], axis=1)
            vmat = v_ref[:, cols].astype(BF16)
            z = lax.dot_general(qs_ref[:, 2 * hh * SB_HEAD:2 * (hh + 1) * SB_HEAD], kmat,
                                (((1,), (1,)), ((), ())), preferred_element_type=F32)
            lsz, lsn = _log_sigmoid_pair(z)
            if masked:
                lsn = jnp.where(causal, lsn, 0.0)
            carry = carry_ref[hh]
            w_parts = [None] * (tile // n_sub)
            for sub in reversed(range(tile // n_sub)):
                sl = slice(sub * n_sub, (sub + 1) * n_sub)
                w = jnp.exp(lsz[:, sl] + _suffix_sum(lsn[:, sl], tri_ref) + carry)
                if masked:
                    w = jnp.where(causal[:, sl], w, 0.0)
                w_parts[sub] = w.astype(BF16)
                carry = carry + jnp.sum(lsn[:, sl], axis=1, keepdims=True)
            carry_ref[hh] = carry
            acc_ref[:, cols] += jnp.dot(jnp.concatenate(w_parts, axis=1), vmat,
                                        preferred_element_type=F32)

    @pl.when(diagonal)
    def _():
        for hh in range(n_heads):
            cols = slice(hh * SB_HEAD, (hh + 1) * SB_HEAD)
            qs_ref[:, 2 * hh * SB_HEAD:(2 * hh + 1) * SB_HEAD] = (
                q_ref[:, cols] * (SB_HEAD ** -0.5)).astype(BF16)
            qs_ref[:, (2 * hh + 1) * SB_HEAD:2 * (hh + 1) * SB_HEAD] = jnp.broadcast_to(
                bias_ref[hh], (tile, SB_HEAD))
        acc_ref[...] = jnp.zeros_like(acc_ref)
        carry_ref[...] = jnp.zeros_like(carry_ref)
        visit(True)

    @pl.when(jnp.logical_not(diagonal))
    def _():
        visit(False)

    @pl.when(kb_ref[s] == 0)
    def _():
        for hh in range(n_heads):
            cols = slice(hh * SB_HEAD, (hh + 1) * SB_HEAD)
            o_ref[:, cols] = _rms(acc_ref[:, cols], g_ref[hh]).astype(o_ref.dtype)


def _sb_schedule(n_tiles):
    qi = [i for i in range(n_tiles) for _ in range(i + 1)]
    kb = [j for i in range(n_tiles) for j in range(i, -1, -1)]
    return [jnp.asarray(np.asarray(a, np.int32)) for a in (qi, kb)]


def _sb_prompt(qkv, q_col, bias, gain, *, heads):
    t_len = qkv.shape[0]
    tile = _tile(t_len, SB_TILE)
    sched = _sb_schedule(t_len // tile)
    nsteps = sched[0].shape[0]
    sub = min(SB_SUB, tile)
    hps = SB_HEADS_PER_STEP
    assert heads % hps == 0 and q_col % hps == 0
    wide = hps * SB_HEAD
    qc = q_col // hps
    b_hi = bias.astype(BF16)
    b_lo = (bias - b_hi.astype(F32)).astype(BF16)
    lane = lax.broadcasted_iota(jnp.int32, (heads, 1, SB_HEAD), 2)
    bias_ext = jnp.where(lane == 0, b_hi[:, None, None],
                         jnp.where(lane == 1, b_lo[:, None, None], jnp.zeros((), BF16)))
    ones_ext = (lax.broadcasted_iota(jnp.int32, (tile, SB_HEAD), 1) < 2).astype(BF16)
    grid_spec = pltpu.PrefetchScalarGridSpec(
        num_scalar_prefetch=2,
        grid=(heads // hps, nsteps),
        in_specs=[
            pl.BlockSpec((tile, wide), lambda h, s, qi, kb: (qi[s], qc + h)),
            pl.BlockSpec((tile, wide), lambda h, s, qi, kb: (kb[s], qc + heads // hps + h)),
            pl.BlockSpec((tile, wide), lambda h, s, qi, kb: (kb[s], qc + 2 * (heads // hps) + h)),
            pl.BlockSpec((sub, sub), lambda h, s, qi, kb: (0, 0)),
            pl.BlockSpec((hps, 1, SB_HEAD), lambda h, s, qi, kb: (h, 0, 0)),
            pl.BlockSpec((hps, 1, SB_HEAD), lambda h, s, qi, kb: (h, 0, 0)),
            pl.BlockSpec((tile, SB_HEAD), lambda h, s, qi, kb: (0, 0)),
        ],
        out_specs=pl.BlockSpec((tile, wide), lambda h, s, qi, kb: (qi[s], h)),
        scratch_shapes=[pltpu.VMEM((tile, 2 * wide), BF16), pltpu.VMEM((tile, wide), F32),
                        pltpu.VMEM((hps, tile, 1), F32)],
    )
    return pl.pallas_call(
        _sbp_body,
        grid_spec=grid_spec,
        out_shape=jax.ShapeDtypeStruct((t_len, heads * SB_HEAD), BF16),
        compiler_params=_cparams(("parallel", "arbitrary")),
        name="sb_prompt",
    )(*sched, qkv, qkv, qkv, _tri2(sub)[:sub], gain.reshape(heads, 1, SB_HEAD), bias_ext, ones_ext)


SBS_PAGES = 4


def _sbs_body(pt_ref, q_ref, *refs, heads, pages):
    k_refs, v_refs = refs[:pages], refs[pages:2 * pages]
    tri_ref, bias_ref, g_ref, o_ref, qs_ref, acc_ref, carry_ref = refs[2 * pages:]
    p = pl.program_id(1)
    rows = k_refs[0].shape[1] * heads
    n_sub = tri_ref.shape[1]
    subs = list(reversed(range(rows // n_sub)))

    @pl.when(p == 0)
    def _():
        qs_ref[...] = (q_ref[0] * (SB_HEAD ** -0.5)).astype(BF16)
        acc_ref[...] = jnp.zeros_like(acc_ref)
        carry_ref[...] = jnp.zeros_like(carry_ref)

    z2_blocks = []
    for k_ref in k_refs:
        kmat = k_ref[0].reshape(rows, SB_HEAD).astype(BF16)
        z2 = lax.dot_general(qs_ref[...], kmat, (((1,), (1,)), ((), ())),
                             preferred_element_type=F32) + bias_ref[...]
        z2_blocks += [z2[:, u * n_sub:(u + 1) * n_sub] for u in subs]
    stacked = (len(z2_blocks) * heads, n_sub)
    own_head = (lax.broadcasted_iota(jnp.int32, stacked, 1) % heads
                == lax.broadcasted_iota(jnp.int32, stacked, 0) % heads)
    lsz, lsn = _log_sigmoid_pair(jnp.concatenate(z2_blocks, axis=0))
    weights, carry_ref[...] = _sb_weights(lsz, jnp.where(own_head, lsn, 0.0), len(z2_blocks),
                                          tri_ref, carry_ref[...], own_head)
    acc = acc_ref[...]
    for u, v_ref in enumerate(v_refs):
        vmat = v_ref[0].reshape(rows, SB_HEAD).astype(BF16)
        w_page = weights[u * len(subs):(u + 1) * len(subs)][::-1]
        acc = acc + jnp.dot(jnp.concatenate(w_page, axis=1), vmat, preferred_element_type=F32)
    acc_ref[...] = acc

    @pl.when(p == pl.num_programs(1) - 1)
    def _():
        o_ref[0] = _rms(acc_ref[...], g_ref[...]).astype(o_ref.dtype)


def _sb_sample(q, cache_k, cache_v, page_table, bias, gain, *, heads):
    nb = q.shape[0]
    n_pages = page_table.shape[1]
    page = cache_k.shape[1]
    sub = min(SB_SUB, page * heads)
    pages = SBS_PAGES
    while n_pages % pages:
        pages //= 2

    def kv_spec(u):
        return pl.BlockSpec((1, page, heads, SB_HEAD),
                            lambda b, p, pt: (pt[b, n_pages - 1 - (p * pages + u)], 0, 0, 0))

    kv_specs = [kv_spec(u) for u in range(pages)]
    grid_spec = pltpu.PrefetchScalarGridSpec(
        num_scalar_prefetch=1,
        grid=(nb, n_pages // pages),
        in_specs=[pl.BlockSpec((1, heads, SB_HEAD), lambda b, p, pt: (b, 0, 0))]
                 + kv_specs + kv_specs
                 + [pl.BlockSpec((2 * sub, sub), lambda b, p, pt: (0, 0)),
                    pl.BlockSpec((heads, 1), lambda b, p, pt: (0, 0)),
                    pl.BlockSpec((heads, SB_HEAD), lambda b, p, pt: (0, 0))],
        out_specs=pl.BlockSpec((1, heads, SB_HEAD), lambda b, p, pt: (b, 0, 0)),
        scratch_shapes=[pltpu.VMEM((heads, SB_HEAD), BF16), pltpu.VMEM((heads, SB_HEAD), F32),
                        pltpu.VMEM((heads, 1), F32)],
    )
    return pl.pallas_call(
        functools.partial(_sbs_body, heads=heads, pages=pages),
        grid_spec=grid_spec,
        out_shape=jax.ShapeDtypeStruct((nb, heads, SB_HEAD), BF16),
        compiler_params=_cparams(("parallel", "arbitrary")),
        name="sb_sample",
    )(page_table, q, *([cache_k] * pages), *([cache_v] * pages), _tri2(sub),
      bias.reshape(heads, 1), gain)


def _perm_cols(w, heads):
    lead = w.shape[:-1]
    return w.reshape(lead + (heads, RWKV_HEAD)).swapaxes(-1, -2).reshape(lead + (heads * RWKV_HEAD,))


def _unperm_cols(w, heads):
    lead = w.shape[:-1]
    return w.reshape(lead + (RWKV_HEAD, heads)).swapaxes(-1, -2).reshape(lead + (heads * RWKV_HEAD,))


def _state_in(s):
    heads = s.shape[-3]
    lead = s.shape[:-3]
    s = jnp.moveaxis(s, -3, -1)
    s = jnp.swapaxes(s, -3, -2)
    return s.reshape(lead + (RWKV_HEAD, RWKV_HEAD * heads // LANES, LANES))


def _state_out(s, heads):
    lead = s.shape[:-3]
    s = s.reshape(lead + (RWKV_HEAD, RWKV_HEAD, heads))
    s = jnp.swapaxes(s, -3, -2)
    return jnp.moveaxis(s, -1, -3)


def _layer(x, pe, shift_prev, wkv0, attend, wts, *, seq_shift):
    heads = wts["heads"]
    width = heads * RWKV_HEAD
    sb_heads = wts["sb_heads"]
    sb_width = sb_heads * SB_HEAD
    m = x.shape[0]

    n_lora = DECAY_LORA + AAA_LORA + GATE_LORA
    h = _norm_cast(x, wts["norm_mix_pre"])
    feats = _matmul(h, wts["w_rkv"], name="mm_in")
    qkv = _matmul(h, wts["w_qkv"], name="mm_qkv")
    lora = _matmul(h, wts["w_lora"], name="mm_lora")

    rows = slice(m - 1, m) if seq_shift else slice(0, m)
    shift_new = jnp.concatenate(
        [_unperm_cols(feats[rows].reshape(-1, 3, width), heads).reshape(-1, 3 * width),
         lora[rows, :n_lora]], axis=1)
    tm = _tile(m, 64) if seq_shift else m
    outs = _rwkv_prep(feats, lora, shift_prev[:, :3 * width], shift_prev[:, 3 * width:], wts,
                      seq_shift=seq_shift, heads=heads, tm=tm)
    e_list, (v_c, bonus_c, g_c) = outs[:5], outs[5:]
    v_rows = v_c.reshape(m, width // LANES, LANES)
    if seq_shift:
        y, s_fin = _wkv_scan_seq(_state_in(wkv0[0]), e_list, v_rows)
        wkv_new = _state_out(s_fin, heads)[None]
    else:
        y, s_fin = _wkv_scan_batch(_state_in(wkv0), e_list, v_rows)
        wkv_new = _state_out(s_fin, heads)
    r_out = _rwkv_post(y.reshape(m, width), bonus_c, g_c, wts, heads=heads)

    a_out = attend(qkv)
    kh = qkv[:, sb_width:2 * sb_width].reshape(m, sb_heads, SB_HEAD)
    vh = qkv[:, 2 * sb_width:].reshape(m, sb_heads, SB_HEAD)

    mix = _matmul_cat(r_out, a_out, wts["w_out_r"], wts["w_out_a"])
    x1, h2 = _resid_norm(x, mix, wts["norm_mix_post"], wts["norm_ffn_pre"])
    up = _matmul(h2, wts["w_up"], out_dtype=BF16, relu2=True, name="mm_up")
    f = _matmul(up, wts["w_down"], name="mm_down")
    x2, x2b = _resid_norm(x1, f, wts["norm_ffn_post"], None)
    out = _ple(x2b, wts["w_ple_gate"], x2, pe.astype(BF16), wts["w_ple_proj"])
    return out, shift_new, wkv_new, kh, vh


def _perm3(w, width, heads):
    lead = w.shape[:-1]
    return _perm_cols(w.reshape(lead + (3, width)), heads).reshape(lead + (3 * width,))


def _shift_in(s, width, heads):
    pad = LORA_PAD - (s.shape[1] - 3 * width)
    return jnp.concatenate([_perm3(s[:, :3 * width], width, heads), s[:, 3 * width:],
                            jnp.zeros((s.shape[0], pad), s.dtype)], axis=1)


def kernel(x_prompt, x_sample, p_prompt, p_sample, state_rwkv_shift, state_rwkv_wkv, cache_k, cache_v, page_table, norm_mix_pre, norm_mix_post, norm_ffn_pre, norm_ffn_post, w_in, rwkv_mu, rwkv_w0, rwkv_w2, rwkv_a0, rwkv_a2, rwkv_g2, rwkv_k_k, rwkv_k_a, rwkv_r_k, rwkv_ln_w, rwkv_ln_b, sb_norm, sb_bias, w_out, w_up, w_down, w_ple_gate, w_ple_proj):
    depth = w_in.shape[0]
    heads = rwkv_r_k.shape[1]
    width = heads * RWKV_HEAD
    sb_heads = sb_bias.shape[1]
    sb_width = sb_heads * SB_HEAD
    nb, t_len, d_model = x_prompt.shape
    db = x_sample.shape[0]
    assert nb == 1 and x_sample.shape[1] == 1 and LANES == 4 * heads
    rwkv_proj = rwkv_mu.shape[1]
    n_lora = rwkv_proj - 3 * width
    assert n_lora == DECAY_LORA + AAA_LORA + GATE_LORA and sb_width + width == w_out.shape[1]

    yp, ys = x_prompt[0], x_sample[:, 0]
    res = [[] for _ in range(8)]
    for i in range(depth):
        wi = w_in[i]
        lpad = jnp.zeros((d_model, LORA_PAD - n_lora), wi.dtype)
        mu = rwkv_mu[i]
        a2p = jnp.concatenate([rwkv_a2[i], jnp.zeros((DECAY_LORA - AAA_LORA, width), F32)], axis=0)
        g2p = jnp.concatenate([jnp.zeros((AAA_LORA, width), F32), rwkv_g2[i],
                               jnp.zeros((LORA_PAD - n_lora, width), F32)], axis=0)
        wo = w_out[i]
        wts = dict(
            heads=heads, sb_heads=sb_heads,
            norm_mix_pre=norm_mix_pre[i], norm_mix_post=norm_mix_post[i],
            norm_ffn_pre=norm_ffn_pre[i], norm_ffn_post=norm_ffn_post[i],
            w_rkv=_perm3(wi[:, :3 * width], width, heads).astype(BF16),
            w_qkv=wi[:, rwkv_proj:].astype(BF16),
            w_lora=jnp.concatenate([wi[:, 3 * width:rwkv_proj], lpad], axis=1).astype(BF16),
            mu_m=_perm3(mu[:3 * width], width, heads).reshape(1, -1),
            mu_l=jnp.concatenate([mu[3 * width:], jnp.zeros((LORA_PAD - n_lora,), F32)]).reshape(1, -1),
            w0=_perm_cols(rwkv_w0[i], heads).reshape(1, -1),
            w2=_perm_cols(rwkv_w2[i], heads).astype(BF16),
            a0=_perm_cols(rwkv_a0[i], heads).reshape(1, -1),
            a2=_perm_cols(a2p, heads).astype(BF16),
            g2=_perm_cols(g2p, heads).astype(BF16),
            k_k=_perm_cols(rwkv_k_k[i], heads).reshape(1, -1),
            k_a=_perm_cols(rwkv_k_a[i], heads).reshape(1, -1),
            r_k=_perm_cols(rwkv_r_k[i].reshape(-1), heads).reshape(1, -1),
            ln_w=_perm_cols(rwkv_ln_w[i], heads).reshape(1, -1),
            ln_b=_perm_cols(rwkv_ln_b[i], heads).reshape(1, -1),
            w_out_r=wo[:width].astype(BF16).reshape(heads, RWKV_HEAD, -1).swapaxes(0, 1)
                    .reshape(width, -1),
            w_out_a=wo[width:].astype(BF16),
            w_up=w_up[i].astype(BF16), w_down=w_down[i].astype(BF16),
            w_ple_gate=w_ple_gate[i].astype(BF16), w_ple_proj=w_ple_proj[i].astype(BF16),
        )
        bias, gain = sb_bias[i], sb_norm[i]

        attend_p = lambda qkv: _sb_prompt(qkv, 0, bias, gain, heads=sb_heads)
        zero_shift = _shift_in(jnp.zeros((1, rwkv_proj), F32), width, heads)
        zero_wkv = jnp.zeros((1, heads, RWKV_HEAD, RWKV_HEAD), F32)
        yp, sp, wp, kp, vp = _layer(yp, p_prompt[i, 0], zero_shift, zero_wkv, attend_p, wts,
                                    seq_shift=True)

        n_phys = cache_k.shape[1]
        ck = cache_k.reshape((depth * n_phys,) + cache_k.shape[2:])
        cv = cache_v.reshape((depth * n_phys,) + cache_v.shape[2:])
        pages_i = page_table + i * n_phys
        attend_s = lambda qkv: _sb_sample(
            qkv[:, :sb_width].reshape(db, sb_heads, SB_HEAD), ck, cv, pages_i, bias, gain,
            heads=sb_heads).reshape(db, sb_width)
        ys, ss, ws, kn, vn = _layer(ys, p_sample[i, :, 0], _shift_in(state_rwkv_shift[i], width, heads),
                                    state_rwkv_wkv[i], attend_s, wts, seq_shift=False)
        for lst, val in zip(res, (sp, ss, wp, ws, kp[None], vp[None], kn[:, None], vn[:, None])):
            lst.append(val)
    return (yp[None], ys[:, None]) + tuple(jnp.stack(r) for r in res)
```

```python
import functools

import jax
import jax.numpy as jnp
import numpy as np
from jax import lax
from jax.experimental import pallas as pl
from jax.experimental.pallas import tpu as pltpu

F32 = jnp.float32
BF16 = jnp.bfloat16

RMS_EPS = 1e-6
GN_EPS = 64e-5
L2_EPS = 1e-12

LANES = 128
RWKV_HEAD = 64
SB_HEAD = 128
DECAY_LORA = 128
AAA_LORA = 96
GATE_LORA = 256
LORA_PAD = 512
VMEM_LIMIT = 52 * 1024 * 1024


def _cparams(sem, vmem=VMEM_LIMIT):
    return pltpu.CompilerParams(dimension_semantics=sem, vmem_limit_bytes=vmem)


def _tile(n, pref):
    if n <= pref:
        return n
    t = pref
    while n % t:
        t //= 2
    return t


def _mm_body(a_ref, b_ref, o_ref, acc_ref, *, nk, relu2):
    def finish(acc):
        if relu2:
            acc = jnp.square(jnp.maximum(acc, 0.0))
        o_ref[...] = acc.astype(o_ref.dtype)

    if nk == 1:
        finish(jnp.dot(a_ref[...], b_ref[...], preferred_element_type=F32))
    else:
        k = pl.program_id(2)

        @pl.when(k == 0)
        def _():
            acc_ref[...] = jnp.zeros_like(acc_ref)

        acc_ref[...] += jnp.dot(a_ref[...], b_ref[...], preferred_element_type=F32)

        @pl.when(k == nk - 1)
        def _():
            finish(acc_ref[...])


def _mm_vmem_bytes(tm, tn, tk, nk, out_dtype):
    out_b = tm * tn * jnp.dtype(out_dtype).itemsize
    acc_b = 3 * tm * tn * 4 if nk > 1 else 0
    return 2 * (2 * tm * tk + 2 * tk * tn + out_b) + acc_b + 2 * 1024 * 1024


def _matmul(a, b, *, out_dtype=F32, relu2=False, tm=1024, tn=1024, tk=2048, name="mm"):
    m, kdim = a.shape
    n = b.shape[1]
    tm, tn = _tile(m, tm), _tile(n, tn)
    tk = kdim if kdim <= 4096 else _tile(kdim, tk)
    nk = kdim // tk
    return pl.pallas_call(
        functools.partial(_mm_body, nk=nk, relu2=relu2),
        grid=(n // tn, m // tm, nk),
        in_specs=[pl.BlockSpec((tm, tk), lambda j, i, k: (i, k)),
                  pl.BlockSpec((tk, tn), lambda j, i, k: (k, j))],
        out_specs=pl.BlockSpec((tm, tn), lambda j, i, k: (i, j)),
        out_shape=jax.ShapeDtypeStruct((m, n), out_dtype),
        scratch_shapes=[pltpu.VMEM((tm, tn) if nk > 1 else (8, LANES), F32)],
        compiler_params=_cparams(("parallel", "parallel", "arbitrary"),
                                 max(VMEM_LIMIT, _mm_vmem_bytes(tm, tn, tk, nk, out_dtype))),
        name=name,
    )(a, b)


def _mm2_body(a1_ref, a2_ref, b1_ref, b2_ref, o_ref):
    o_ref[...] = (jnp.dot(a1_ref[...], b1_ref[...], preferred_element_type=F32)
                  + jnp.dot(a2_ref[...], b2_ref[...], preferred_element_type=F32))


def _matmul_cat(a1, a2, b1, b2, *, tm=1024, tn=1024):
    m, k1 = a1.shape
    k2 = a2.shape[1]
    n = b1.shape[1]
    tm, tn = _tile(m, tm), _tile(n, tn)
    return pl.pallas_call(
        _mm2_body,
        grid=(n // tn, m // tm),
        in_specs=[pl.BlockSpec((tm, k1), lambda j, i: (i, 0)),
                  pl.BlockSpec((tm, k2), lambda j, i: (i, 0)),
                  pl.BlockSpec((k1, tn), lambda j, i: (0, j)),
                  pl.BlockSpec((k2, tn), lambda j, i: (0, j))],
        out_specs=pl.BlockSpec((tm, tn), lambda j, i: (i, j)),
        out_shape=jax.ShapeDtypeStruct((m, n), F32),
        compiler_params=_cparams(("parallel", "parallel")),
        name="mm_out",
    )(a1, a2, b1, b2)


def _ple_body(a_ref, b_ref, x_ref, pe_ref, wp_ref, o_ref):
    gate = jnp.dot(a_ref[...], b_ref[...], preferred_element_type=F32)
    proj = jnp.dot(pe_ref[...], wp_ref[...], preferred_element_type=F32)
    o_ref[...] = x_ref[...] + jax.nn.sigmoid(gate) * proj


def _ple(xb, wg, x, pe, wp, *, tm=512, tn=1024):
    m, kdim = xb.shape
    n = wg.shape[1]
    kp = pe.shape[1]
    tm, tn = _tile(m, tm), _tile(n, tn)
    return pl.pallas_call(
        _ple_body,
        grid=(n // tn, m // tm),
        in_specs=[pl.BlockSpec((tm, kdim), lambda j, i: (i, 0)),
                  pl.BlockSpec((kdim, tn), lambda j, i: (0, j)),
                  pl.BlockSpec((tm, tn), lambda j, i: (i, j)),
                  pl.BlockSpec((tm, kp), lambda j, i: (i, 0)),
                  pl.BlockSpec((kp, tn), lambda j, i: (0, j))],
        out_specs=pl.BlockSpec((tm, tn), lambda j, i: (i, j)),
        out_shape=jax.ShapeDtypeStruct((m, n), F32),
        compiler_params=_cparams(("parallel", "parallel")),
        name="ple",
    )(xb, wg, x, pe, wp)


def _rms(x, g):
    return x * lax.rsqrt(jnp.mean(x * x, axis=-1, keepdims=True) + RMS_EPS) * g


def _norm_cast_body(x_ref, g_ref, o_ref):
    o_ref[...] = _rms(x_ref[...], g_ref[...]).astype(o_ref.dtype)


def _norm_cast(x, g, *, tm=256):
    m, d = x.shape
    tm = _tile(m, tm)
    return pl.pallas_call(
        _norm_cast_body,
        grid=(m // tm,),
        in_specs=[pl.BlockSpec((tm, d), lambda i: (i, 0)),
                  pl.BlockSpec((1, d), lambda i: (0, 0))],
        out_specs=pl.BlockSpec((tm, d), lambda i: (i, 0)),
        out_shape=jax.ShapeDtypeStruct((m, d), BF16),
        compiler_params=_cparams(("parallel",)),
        name="norm_cast",
    )(x, g.reshape(1, d))


def _resid_body(x_ref, f_ref, g_ref, gn_ref, xo_ref, no_ref, *, norm_next):
    xn = x_ref[...] + _rms(f_ref[...], g_ref[...])
    xo_ref[...] = xn
    if norm_next:
        no_ref[...] = _rms(xn, gn_ref[...]).astype(no_ref.dtype)
    else:
        no_ref[...] = xn.astype(no_ref.dtype)


def _resid_norm(x, f, g, g_next, *, tm=256):
    m, d = x.shape
    tm = _tile(m, tm)
    norm_next = g_next is not None
    gn = g_next if norm_next else g
    row = pl.BlockSpec((tm, d), lambda i: (i, 0))
    vec = pl.BlockSpec((1, d), lambda i: (0, 0))
    return pl.pallas_call(
        functools.partial(_resid_body, norm_next=norm_next),
        grid=(m // tm,),
        in_specs=[row, row, vec, vec],
        out_specs=[row, row],
        out_shape=[jax.ShapeDtypeStruct((m, d), F32), jax.ShapeDtypeStruct((m, d), BF16)],
        compiler_params=_cparams(("parallel",)),
        name="resid_norm",
    )(x, f, g.reshape(1, d), gn.reshape(1, d))


def _head_allsum(p, heads):
    q = p + pltpu.roll(p, 2 * heads, axis=1)
    return q + pltpu.roll(q, heads, axis=1)


def _col_slices(x):
    return [x[:, i * LANES:(i + 1) * LANES] for i in range(x.shape[1] // LANES)]


def _split2(x):
    hi = x.astype(BF16)
    return hi, (x - hi.astype(F32)).astype(BF16)


def _group_select(heads):
    src = np.arange(LANES)[:, None]
    dst = np.arange(2 * LANES)[None, :]
    mats = [(src == (2 * p + dst // LANES) * heads + dst % heads) for p in range(2)]
    return jnp.asarray(np.stack(mats), BF16)


def _prep_body(fm_ref, fl_ref, pm_ref, plo_ref, spm_ref, spl_ref, mum_ref, mul_ref, w0_ref, w2_ref,
               a0_ref, a2_ref, g2_ref, kk_ref, ka_ref, rk_ref, sel_ref,
               e_nkk, e_d, e_b, e_k, e_r, v_out, bonus_out, g_out, *, seq_shift, heads):
    width = heads * RWKV_HEAD

    def shifted(x, p_ref, sp_ref):
        if not seq_shift:
            return p_ref[...]
        before = jnp.where(pl.program_id(0) == 0, sp_ref[...], p_ref[7:8, :])
        row = lax.broadcasted_iota(jnp.int32, x.shape, 0)
        return jnp.where(row == 0, before, pltpu.roll(x, 1, axis=0))

    x = fm_ref[...]
    xm = x + (shifted(x, pm_ref, spm_ref) - x) * mum_ref[...]
    lo = fl_ref[...]
    lm = lo + (shifted(lo, plo_ref, spl_ref) - lo) * mul_ref[...]

    r = xm[:, :width]
    k = xm[:, width:2 * width]
    v = xm[:, 2 * width:]

    u = w0_ref[...] + jnp.dot(jnp.tanh(lm[:, :DECAY_LORA]).astype(BF16), w2_ref[...],
                              preferred_element_type=F32)
    w_log = -(jnp.maximum(-u, 0.0) + jnp.log1p(jnp.exp(-jnp.abs(u)))) - 0.5
    decay = jnp.exp(-jnp.exp(w_log))
    a = jax.nn.sigmoid(a0_ref[...] + jnp.dot(lm[:, DECAY_LORA:2 * DECAY_LORA].astype(BF16),
                                             a2_ref[...], preferred_element_type=F32))
    g = jnp.dot(jax.nn.sigmoid(lm[:, DECAY_LORA:]).astype(BF16), g2_ref[...],
                preferred_element_type=F32)

    kk = k * kk_ref[...]
    sq = _col_slices(kk * kk)
    tot = sq[0]
    for s in sq[1:]:
        tot = tot + s
    den = jnp.maximum(jnp.sqrt(_head_allsum(tot, heads)), L2_EPS)
    kk = jnp.concatenate([c / den for c in _col_slices(kk)], axis=1)
    kf = k * (1.0 + (a - 1.0) * ka_ref[...])
    b = kk * a

    rk_parts = _col_slices(r * kf * rk_ref[...])
    rk_tot = rk_parts[0]
    for s in rk_parts[1:]:
        rk_tot = rk_tot + s
    rk_tot = _head_allsum(rk_tot, heads)
    v_out[...] = v
    bonus_out[...] = jnp.concatenate([rk_tot * c for c in _col_slices(v)], axis=1)
    g_out[...] = g

    rows = x.shape[0]

    def expand(val, ref):
        hi, lo = _split2(jnp.concatenate(_col_slices(val), axis=0))
        for p in range(2):
            sel = sel_ref[p]
            out = (jnp.dot(hi, sel, preferred_element_type=F32)
                   + jnp.dot(lo, sel, preferred_element_type=F32))
            for i in range(width // LANES):
                for q in range(2):
                    ref[4 * i + 2 * p + q] = out[i * rows:(i + 1) * rows, q * LANES:(q + 1) * LANES]

    expand(-kk, e_nkk)
    expand(decay, e_d)
    expand(b, e_b)
    expand(kf, e_k)
    expand(r, e_r)


def _rwkv_prep(feats, lora, shift_m, shift_l, prm, *, seq_shift, heads, tm):
    m = feats.shape[0]
    width = heads * RWKV_HEAD
    tm = _tile(m, tm)
    nt = m // tm
    if seq_shift:
        assert tm % 8 == 0
        prev_m, prev_l = feats, lora
        pm_spec = pl.BlockSpec((8, 3 * width), lambda i: (jnp.maximum(i * (tm // 8) - 1, 0), 0))
        pl_spec = pl.BlockSpec((8, LORA_PAD), lambda i: (jnp.maximum(i * (tm // 8) - 1, 0), 0))
    else:
        prev_m, prev_l = shift_m, shift_l
        pm_spec = pl.BlockSpec((tm, 3 * width), lambda i: (i, 0))
        pl_spec = pl.BlockSpec((tm, LORA_PAD), lambda i: (i, 0))

    def vec(n):
        return pl.BlockSpec((1, n), lambda i: (0, 0))

    def full(r_, c_):
        return pl.BlockSpec((r_, c_), lambda i: (0, 0))

    e_spec = pl.BlockSpec((RWKV_HEAD, tm, LANES), lambda i: (0, i, 0))
    c_spec = pl.BlockSpec((tm, width), lambda i: (i, 0))
    e_shape = jax.ShapeDtypeStruct((RWKV_HEAD, m, LANES), F32)
    c_shape = jax.ShapeDtypeStruct((m, width), F32)
    return pl.pallas_call(
        functools.partial(_prep_body, seq_shift=seq_shift, heads=heads),
        grid=(nt,),
        in_specs=[pl.BlockSpec((tm, 3 * width), lambda i: (i, 0)),
                  pl.BlockSpec((tm, LORA_PAD), lambda i: (i, 0)),
                  pm_spec, pl_spec, vec(3 * width), vec(LORA_PAD), vec(3 * width), vec(LORA_PAD),
                  vec(width), full(DECAY_LORA, width), vec(width), full(DECAY_LORA, width),
                  full(LORA_PAD - DECAY_LORA, width), vec(width), vec(width), vec(width),
                  pl.BlockSpec((2, LANES, 2 * LANES), lambda i: (0, 0, 0))],
        out_specs=[e_spec] * 5 + [c_spec] * 3,
        out_shape=[e_shape] * 5 + [c_shape] * 3,
        compiler_params=_cparams(("parallel",)),
        name="rwkv_prep",
    )(feats, lora, prev_m, prev_l, shift_m[:1], shift_l[:1], prm["mu_m"], prm["mu_l"], prm["w0"],
      prm["w2"], prm["a0"], prm["a2"], prm["g2"], prm["k_k"], prm["k_a"], prm["r_k"],
      _group_select(heads))


N_ACC = 4


def _tree_sum(parts):
    while len(parts) > 1:
        parts = [parts[i] + parts[i + 1] for i in range(0, len(parts), 2)]
    return parts[0]


def _wkv_sa(s_ref, nkk_ref, t):
    acc = [None] * N_ACC
    for kx in range(RWKV_HEAD):
        term = s_ref[kx] * nkk_ref[kx, pl.ds(t, 1), :]
        acc[kx % N_ACC] = term if acc[kx % N_ACC] is None else acc[kx % N_ACC] + term
    return _tree_sum(acc)


def _wkv_step(s_ref, nkk_ref, d_ref, b_ref, k_ref, r_ref, v_t, sa, t, t_next):
    yacc = [None] * N_ACC
    sacc = [None] * N_ACC
    for kx in range(RWKV_HEAD):
        s_new = (s_ref[kx] * d_ref[kx, pl.ds(t, 1), :] + sa * b_ref[kx, pl.ds(t, 1), :]
                 + v_t * k_ref[kx, pl.ds(t, 1), :])
        s_ref[kx] = s_new
        term = s_new * r_ref[kx, pl.ds(t, 1), :]
        yacc[kx % N_ACC] = term if yacc[kx % N_ACC] is None else yacc[kx % N_ACC] + term
        if t_next is not None:
            term = s_new * nkk_ref[kx, pl.ds(t_next, 1), :]
            sacc[kx % N_ACC] = term if sacc[kx % N_ACC] is None else sacc[kx % N_ACC] + term
    return _tree_sum(yacc), (None if t_next is None else _tree_sum(sacc))


def _scan_seq_body(s0_ref, nkk_ref, d_ref, b_ref, k_ref, r_ref, v_ref, y_ref, sout_ref, s_ref, *,
                   tc):
    c = pl.program_id(0)

    @pl.when(c == 0)
    def _():
        s_ref[...] = s0_ref[...]

    def step(t, sa):
        y_ref[t], sa_next = _wkv_step(s_ref, nkk_ref, d_ref, b_ref, k_ref, r_ref, v_ref[t], sa, t,
                                      jnp.minimum(t + 1, tc - 1))
        return sa_next

    lax.fori_loop(0, tc, step, _wkv_sa(s_ref, nkk_ref, 0), unroll=8)

    @pl.when(c == pl.num_programs(0) - 1)
    def _():
        sout_ref[...] = s_ref[...]


def _wkv_scan_seq(s0, e_list, v, *, tc=64):
    t_len = v.shape[0]
    tc = _tile(t_len, tc)
    s_spec = pl.BlockSpec(s0.shape, lambda c: (0, 0, 0))
    e_spec = pl.BlockSpec((RWKV_HEAD, tc, LANES), lambda c: (0, c, 0))
    v_spec = pl.BlockSpec((tc,) + v.shape[1:], lambda c: (c, 0, 0))
    return pl.pallas_call(
        functools.partial(_scan_seq_body, tc=tc),
        grid=(t_len // tc,),
        in_specs=[s_spec] + [e_spec] * 5 + [v_spec],
        out_specs=[v_spec, s_spec],
        out_shape=[jax.ShapeDtypeStruct(v.shape, F32), jax.ShapeDtypeStruct(s0.shape, F32)],
        scratch_shapes=[pltpu.VMEM(s0.shape, F32)],
        compiler_params=_cparams(("arbitrary",)),
        name="wkv_scan",
    )(s0, *e_list, v)


def _scan_batch_body(s0_ref, nkk_ref, d_ref, b_ref, k_ref, r_ref, v_ref, y_ref, sout_ref):
    bidx = pl.program_id(0)
    s = sout_ref.at[0]
    s[...] = s0_ref[0]
    y_ref[0], _ = _wkv_step(s, nkk_ref, d_ref, b_ref, k_ref, r_ref, v_ref[0],
                            _wkv_sa(s, nkk_ref, bidx), bidx, None)


def _wkv_scan_batch(s0, e_list, v):
    nb = v.shape[0]
    s_spec = pl.BlockSpec((1,) + s0.shape[1:], lambda b: (b, 0, 0, 0))
    e_spec = pl.BlockSpec((RWKV_HEAD, nb, LANES), lambda b: (0, 0, 0))
    v_spec = pl.BlockSpec((1,) + v.shape[1:], lambda b: (b, 0, 0))
    return pl.pallas_call(
        _scan_batch_body,
        grid=(nb,),
        in_specs=[s_spec] + [e_spec] * 5 + [v_spec],
        out_specs=[v_spec, s_spec],
        out_shape=[jax.ShapeDtypeStruct(v.shape, F32), jax.ShapeDtypeStruct(s0.shape, F32)],
        compiler_params=_cparams(("arbitrary",)),
        name="wkv_step",
    )(s0, *e_list, v)


def _post_body(y_ref, bonus_ref, g_ref, lnw_ref, lnb_ref, o_ref, *, heads):
    inv_n = 1.0 / RWKV_HEAD

    def head_sum(x):
        cols = _col_slices(x)
        tot = cols[0]
        for c in cols[1:]:
            tot = tot + c
        return _head_allsum(tot, heads)

    def tiled(stat, like):
        return jnp.concatenate([stat] * (like.shape[1] // LANES), axis=1)

    y = y_ref[...]
    yc = y - tiled(head_sum(y) * inv_n, y)
    var = head_sum(yc * yc) * inv_n
    yn = yc * tiled(lax.rsqrt(var + GN_EPS), y) * lnw_ref[...] + lnb_ref[...]
    o_ref[...] = ((yn + bonus_ref[...]) * g_ref[...]).astype(o_ref.dtype)


def _rwkv_post(y, bonus, g, prm, *, heads, tm=256):
    m, width = y.shape
    tm = _tile(m, tm)
    row = pl.BlockSpec((tm, width), lambda i: (i, 0))
    vec = pl.BlockSpec((1, width), lambda i: (0, 0))
    return pl.pallas_call(
        functools.partial(_post_body, heads=heads),
        grid=(m // tm,),
        in_specs=[row] * 3 + [vec] * 2,
        out_specs=row,
        out_shape=jax.ShapeDtypeStruct((m, width), BF16),
        compiler_params=_cparams(("parallel",)),
        name="rwkv_post",
    )(y, bonus, g, prm["ln_w"], prm["ln_b"])


SB_TILE = 512
SB_SUB = 256
SB_HEADS_PER_STEP = 8
LOG2E = 1.4426950408889634


def _log_sigmoid_pair(z):
    lsz = jnp.minimum(z, 0.0) - jnp.log(1.0 + jnp.exp2(jnp.abs(z) * (-LOG2E)))
    return lsz, lsz - z


def _sb_weights(lsz, lsn, n_blocks, tri2_ref, carry, valid):
    rows = lsz.shape[0] // n_blocks
    hi = lsn.astype(BF16)
    lo = (lsn - hi.astype(F32)).astype(BF16)
    if n_blocks == 1:
        tri = tri2_ref[:lsz.shape[1]]
        logw = lsz + (jnp.dot(hi, tri, preferred_element_type=F32)
                      + jnp.dot(lo, tri, preferred_element_type=F32))
    else:
        logw = lsz + jnp.dot(jnp.concatenate([hi, lo], axis=1), tri2_ref[...],
                             preferred_element_type=F32)
    taken = jnp.sum(lsn, axis=1, keepdims=True)
    weights = []
    for b in range(n_blocks):
        sl = slice(b * rows, (b + 1) * rows)
        w = jnp.exp(logw[sl] + carry)
        if valid is not None:
            w = jnp.where(valid[sl], w, 0.0)
        weights.append(w.astype(BF16))
        carry = carry + taken[sl]
    return weights, carry


def _tri2(n):
    tri = (lax.broadcasted_iota(jnp.int32, (n, n), 0)
           > lax.broadcasted_iota(jnp.int32, (n, n), 1)).astype(BF16)
    return jnp.concatenate([tri, tri], axis=0)


def _suffix_sum(ls, tri_ref):
    hi = ls.astype(BF16)
    lo = (ls - hi.astype(F32)).astype(BF16)
    tri = tri_ref[...]
    return jnp.dot(jnp.concatenate([hi, lo], axis=1), jnp.concatenate([tri, tri], axis=0),
                   preferred_element_type=F32)


def _sbp_body(qi_ref, kb_ref, q_ref, k_ref, v_ref, tri_ref, g_ref, bias_ref, ones_ref,
              o_ref, qs_ref, acc_ref, carry_ref):
    s = pl.program_id(1)
    tile = q_ref.shape[0]
    n_sub = tri_ref.shape[0]
    n_heads = q_ref.shape[1] // SB_HEAD
    diagonal = kb_ref[s] == qi_ref[s]

    def visit(masked):
        if masked:
            causal = (lax.broadcasted_iota(jnp.int32, (tile, tile), 1)
                      < lax.broadcasted_iota(jnp.int32, (tile, tile), 0))
        for hh in range(n_heads):
            cols = slice(hh * SB_HEAD, (hh + 1) * SB_HEAD)
            kmat = jnp.concatenate([k_ref[:, cols].astype(BF16), ones_ref[...]], axis=1)
            vmat = v_ref[:, cols].astype(BF16)
            z = lax.dot_general(qs_ref[:, 2 * hh * SB_HEAD:2 * (hh + 1) * SB_HEAD], kmat,
                                (((1,), (1,)), ((), ())), preferred_element_type=F32)
            lsz, lsn = _log_sigmoid_pair(z)
            if masked:
                lsn = jnp.where(causal, lsn, 0.0)
            carry = carry_ref[hh]
            w_parts = [None] * (tile // n_sub)
            for sub in reversed(range(tile // n_sub)):
                sl = slice(sub * n_sub, (sub + 1) * n_sub)
                w = jnp.exp(lsz[:, sl] + _suffix_sum(lsn[:, sl], tri_ref) + carry)
                if masked:
                    w = jnp.where(causal[:, sl], w, 0.0)
                w_parts[sub] = w.astype(BF16)
                carry = carry + jnp.sum(lsn[:, sl], axis=1, keepdims=True)
            carry_ref[hh] = carry
            acc_ref[:, cols] += jnp.dot(jnp.concatenate(w_parts, axis=1), vmat,
                                        preferred_element_type=F32)

    @pl.when(diagonal)
    def _():
        for hh in range(n_heads):
            cols = slice(hh * SB_HEAD, (hh + 1) * SB_HEAD)
            qs_ref[:, 2 * hh * SB_HEAD:(2 * hh + 1) * SB_HEAD] = (
                q_ref[:, cols] * (SB_HEAD ** -0.5)).astype(BF16)
            qs_ref[:, (2 * hh + 1) * SB_HEAD:2 * (hh + 1) * SB_HEAD] = jnp.broadcast_to(
                bias_ref[hh], (tile, SB_HEAD))
        acc_ref[...] = jnp.zeros_like(acc_ref)
        carry_ref[...] = jnp.zeros_like(carry_ref)
        visit(True)

    @pl.when(jnp.logical_not(diagonal))
    def _():
        visit(False)

    @pl.when(kb_ref[s] == 0)
    def _():
        for hh in range(n_heads):
            cols = slice(hh * SB_HEAD, (hh + 1) * SB_HEAD)
            o_ref[:, cols] = _rms(acc_ref[:, cols], g_ref[hh]).astype(o_ref.dtype)


def _sb_schedule(n_tiles):
    qi = [i for i in range(n_tiles) for _ in range(i + 1)]
    kb = [j for i in range(n_tiles) for j in range(i, -1, -1)]
    return [jnp.asarray(np.asarray(a, np.int32)) for a in (qi, kb)]


def _sb_prompt(qkv, q_col, bias, gain, *, heads):
    t_len = qkv.shape[0]
    tile = _tile(t_len, SB_TILE)
    sched = _sb_schedule(t_len // tile)
    nsteps = sched[0].shape[0]
    sub = min(SB_SUB, tile)
    hps = SB_HEADS_PER_STEP
    assert heads % hps == 0 and q_col % hps == 0
    wide = hps * SB_HEAD
    qc = q_col // hps
    b_hi = bias.astype(BF16)
    b_lo = (bias - b_hi.astype(F32)).astype(BF16)
    lane = lax.broadcasted_iota(jnp.int32, (heads, 1, SB_HEAD), 2)
    bias_ext = jnp.where(lane == 0, b_hi[:, None, None],
                         jnp.where(lane == 1, b_lo[:, None, None], jnp.zeros((), BF16)))
    ones_ext = (lax.broadcasted_iota(jnp.int32, (tile, SB_HEAD), 1) < 2).astype(BF16)
    grid_spec = pltpu.PrefetchScalarGridSpec(
        num_scalar_prefetch=2,
        grid=(heads // hps, nsteps),
        in_specs=[
            pl.BlockSpec((tile, wide), lambda h, s, qi, kb: (qi[s], qc + h)),
            pl.BlockSpec((tile, wide), lambda h, s, qi, kb: (kb[s], qc + heads // hps + h)),
            pl.BlockSpec((tile, wide), lambda h, s, qi, kb: (kb[s], qc + 2 * (heads // hps) + h)),
            pl.BlockSpec((sub, sub), lambda h, s, qi, kb: (0, 0)),
            pl.BlockSpec((hps, 1, SB_HEAD), lambda h, s, qi, kb: (h, 0, 0)),
            pl.BlockSpec((hps, 1, SB_HEAD), lambda h, s, qi, kb: (h, 0, 0)),
            pl.BlockSpec((tile, SB_HEAD), lambda h, s, qi, kb: (0, 0)),
        ],
        out_specs=pl.BlockSpec((tile, wide), lambda h, s, qi, kb: (qi[s], h)),
        scratch_shapes=[pltpu.VMEM((tile, 2 * wide), BF16), pltpu.VMEM((tile, wide), F32),
                        pltpu.VMEM((hps, tile, 1), F32)],
    )
    return pl.pallas_call(
        _sbp_body,
        grid_spec=grid_spec,
        out_shape=jax.ShapeDtypeStruct((t_len, heads * SB_HEAD), BF16),
        compiler_params=_cparams(("parallel", "arbitrary")),
        name="sb_prompt",
    )(*sched, qkv, qkv, qkv, _tri2(sub)[:sub], gain.reshape(heads, 1, SB_HEAD), bias_ext, ones_ext)


SBS_PAGES = 4


def _sbs_body(pt_ref, q_ref, *refs, heads, pages):
    k_refs, v_refs = refs[:pages], refs[pages:2 * pages]
    tri_ref, bias_ref, g_ref, o_ref, qs_ref, acc_ref, carry_ref = refs[2 * pages:]
    p = pl.program_id(1)
    rows = k_refs[0].shape[1] * heads
    n_sub = tri_ref.shape[1]
    subs = list(reversed(range(rows // n_sub)))

    @pl.when(p == 0)
    def _():
        qs_ref[...] = (q_ref[0] * (SB_HEAD ** -0.5)).astype(BF16)
        acc_ref[...] = jnp.zeros_like(acc_ref)
        carry_ref[...] = jnp.zeros_like(carry_ref)

    z2_blocks = []
    for k_ref in k_refs:
        kmat = k_ref[0].reshape(rows, SB_HEAD).astype(BF16)
        z2 = lax.dot_general(qs_ref[...], kmat, (((1,), (1,)), ((), ())),
                             preferred_element_type=F32) + bias_ref[...]
        z2_blocks += [z2[:, u * n_sub:(u + 1) * n_sub] for u in subs]
    stacked = (len(z2_blocks) * heads, n_sub)
    own_head = (lax.broadcasted_iota(jnp.int32, stacked, 1) % heads
                == lax.broadcasted_iota(jnp.int32, stacked, 0) % heads)
    lsz, lsn = _log_sigmoid_pair(jnp.concatenate(z2_blocks, axis=0))
    weights, carry_ref[...] = _sb_weights(lsz, jnp.where(own_head, lsn, 0.0), len(z2_blocks),
                                          tri_ref, carry_ref[...], own_head)
    acc = acc_ref[...]
    for u, v_ref in enumerate(v_refs):
        vmat = v_ref[0].reshape(rows, SB_HEAD).astype(BF16)
        w_page = weights[u * len(subs):(u + 1) * len(subs)][::-1]
        acc = acc + jnp.dot(jnp.concatenate(w_page, axis=1), vmat, preferred_element_type=F32)
    acc_ref[...] = acc

    @pl.when(p == pl.num_programs(1) - 1)
    def _():
        o_ref[0] = _rms(acc_ref[...], g_ref[...]).astype(o_ref.dtype)


def _sb_sample(q, cache_k, cache_v, page_table, bias, gain, *, heads):
    nb = q.shape[0]
    n_pages = page_table.shape[1]
    page = cache_k.shape[1]
    sub = min(SB_SUB, page * heads)
    pages = SBS_PAGES
    while n_pages % pages:
        pages //= 2

    def kv_spec(u):
        return pl.BlockSpec((1, page, heads, SB_HEAD),
                            lambda b, p, pt: (pt[b, n_pages - 1 - (p * pages + u)], 0, 0, 0))

    kv_specs = [kv_spec(u) for u in range(pages)]
    grid_spec = pltpu.PrefetchScalarGridSpec(
        num_scalar_prefetch=1,
        grid=(nb, n_pages // pages),
        in_specs=[pl.BlockSpec((1, heads, SB_HEAD), lambda b, p, pt: (b, 0, 0))]
                 + kv_specs + kv_specs
                 + [pl.BlockSpec((2 * sub, sub), lambda b, p, pt: (0, 0)),
                    pl.BlockSpec((heads, 1), lambda b, p, pt: (0, 0)),
                    pl.BlockSpec((heads, SB_HEAD), lambda b, p, pt: (0, 0))],
        out_specs=pl.BlockSpec((1, heads, SB_HEAD), lambda b, p, pt: (b, 0, 0)),
        scratch_shapes=[pltpu.VMEM((heads, SB_HEAD), BF16), pltpu.VMEM((heads, SB_HEAD), F32),
                        pltpu.VMEM((heads, 1), F32)],
    )
    return pl.pallas_call(
        functools.partial(_sbs_body, heads=heads, pages=pages),
        grid_spec=grid_spec,
        out_shape=jax.ShapeDtypeStruct((nb, heads, SB_HEAD), BF16),
        compiler_params=_cparams(("parallel", "arbitrary")),
        name="sb_sample",
    )(page_table, q, *([cache_k] * pages), *([cache_v] * pages), _tri2(sub),
      bias.reshape(heads, 1), gain)


def _perm_cols(w, heads):
    lead = w.shape[:-1]
    return w.reshape(lead + (heads, RWKV_HEAD)).swapaxes(-1, -2).reshape(lead + (heads * RWKV_HEAD,))


def _unperm_cols(w, heads):
    lead = w.shape[:-1]
    return w.reshape(lead + (RWKV_HEAD, heads)).swapaxes(-1, -2).reshape(lead + (heads * RWKV_HEAD,))


def _state_in(s):
    heads = s.shape[-3]
    lead = s.shape[:-3]
    s = jnp.moveaxis(s, -3, -1)
    s = jnp.swapaxes(s, -3, -2)
    return s.reshape(lead + (RWKV_HEAD, RWKV_HEAD * heads // LANES, LANES))


def _state_out(s, heads):
    lead = s.shape[:-3]
    s = s.reshape(lead + (RWKV_HEAD, RWKV_HEAD, heads))
    s = jnp.swapaxes(s, -3, -2)
    return jnp.moveaxis(s, -1, -3)


def _layer(x, pe, shift_prev, wkv0, attend, wts, *, seq_shift):
    heads = wts["heads"]
    width = heads * RWKV_HEAD
    sb_heads = wts["sb_heads"]
    sb_width = sb_heads * SB_HEAD
    m = x.shape[0]

    n_lora = DECAY_LORA + AAA_LORA + GATE_LORA
    h = _norm_cast(x, wts["norm_mix_pre"])
    feats = _matmul(h, wts["w_rkv"], name="mm_in")
    qkv = _matmul(h, wts["w_qkv"], name="mm_qkv")
    lora = _matmul(h, wts["w_lora"], name="mm_lora")

    rows = slice(m - 1, m) if seq_shift else slice(0, m)
    shift_new = jnp.concatenate(
        [_unperm_cols(feats[rows].reshape(-1, 3, width), heads).reshape(-1, 3 * width),
         lora[rows, :n_lora]], axis=1)
    tm = _tile(m, 64) if seq_shift else m
    outs = _rwkv_prep(feats, lora, shift_prev[:, :3 * width], shift_prev[:, 3 * width:], wts,
                      seq_shift=seq_shift, heads=heads, tm=tm)
    e_list, (v_c, bonus_c, g_c) = outs[:5], outs[5:]
    v_rows = v_c.reshape(m, width // LANES, LANES)
    if seq_shift:
        y, s_fin = _wkv_scan_seq(_state_in(wkv0[0]), e_list, v_rows)
        wkv_new = _state_out(s_fin, heads)[None]
    else:
        y, s_fin = _wkv_scan_batch(_state_in(wkv0), e_list, v_rows)
        wkv_new = _state_out(s_fin, heads)
    r_out = _rwkv_post(y.reshape(m, width), bonus_c, g_c, wts, heads=heads)

    a_out = attend(qkv)
    kh = qkv[:, sb_width:2 * sb_width].reshape(m, sb_heads, SB_HEAD)
    vh = qkv[:, 2 * sb_width:].reshape(m, sb_heads, SB_HEAD)

    mix = _matmul_cat(r_out, a_out, wts["w_out_r"], wts["w_out_a"])
    x1, h2 = _resid_norm(x, mix, wts["norm_mix_post"], wts["norm_ffn_pre"])
    up = _matmul(h2, wts["w_up"], out_dtype=BF16, relu2=True, name="mm_up")
    f = _matmul(up, wts["w_down"], tk=4096, name="mm_down")
    x2, x2b = _resid_norm(x1, f, wts["norm_ffn_post"], None)
    out = _ple(x2b, wts["w_ple_gate"], x2, pe.astype(BF16), wts["w_ple_proj"])
    return out, shift_new, wkv_new, kh, vh


def _perm3(w, width, heads):
    lead = w.shape[:-1]
    return _perm_cols(w.reshape(lead + (3, width)), heads).reshape(lead + (3 * width,))


def _shift_in(s, width, heads):
    pad = LORA_PAD - (s.shape[1] - 3 * width)
    return jnp.concatenate([_perm3(s[:, :3 * width], width, heads), s[:, 3 * width:],
                            jnp.zeros((s.shape[0], pad), s.dtype)], axis=1)


def kernel(x_prompt, x_sample, p_prompt, p_sample, state_rwkv_shift, state_rwkv_wkv, cache_k, cache_v, page_table, norm_mix_pre, norm_mix_post, norm_ffn_pre, norm_ffn_post, w_in, rwkv_mu, rwkv_w0, rwkv_w2, rwkv_a0, rwkv_a2, rwkv_g2, rwkv_k_k, rwkv_k_a, rwkv_r_k, rwkv_ln_w, rwkv_ln_b, sb_norm, sb_bias, w_out, w_up, w_down, w_ple_gate, w_ple_proj):
    depth = w_in.shape[0]
    heads = rwkv_r_k.shape[1]
    width = heads * RWKV_HEAD
    sb_heads = sb_bias.shape[1]
    sb_width = sb_heads * SB_HEAD
    nb, t_len, d_model = x_prompt.shape
    db = x_sample.shape[0]
    assert nb == 1 and x_sample.shape[1] == 1 and LANES == 4 * heads
    rwkv_proj = rwkv_mu.shape[1]
    n_lora = rwkv_proj - 3 * width
    assert n_lora == DECAY_LORA + AAA_LORA + GATE_LORA and sb_width + width == w_out.shape[1]

    yp, ys = x_prompt[0], x_sample[:, 0]
    res = [[] for _ in range(8)]
    for i in range(depth):
        wi = w_in[i]
        lpad = jnp.zeros((d_model, LORA_PAD - n_lora), wi.dtype)
        mu = rwkv_mu[i]
        a2p = jnp.concatenate([rwkv_a2[i], jnp.zeros((DECAY_LORA - AAA_LORA, width), F32)], axis=0)
        g2p = jnp.concatenate([jnp.zeros((AAA_LORA, width), F32), rwkv_g2[i],
                               jnp.zeros((LORA_PAD - n_lora, width), F32)], axis=0)
        wo = w_out[i]
        wts = dict(
            heads=heads, sb_heads=sb_heads,
            norm_mix_pre=norm_mix_pre[i], norm_mix_post=norm_mix_post[i],
            norm_ffn_pre=norm_ffn_pre[i], norm_ffn_post=norm_ffn_post[i],
            w_rkv=_perm3(wi[:, :3 * width], width, heads).astype(BF16),
            w_qkv=wi[:, rwkv_proj:].astype(BF16),
            w_lora=jnp.concatenate([wi[:, 3 * width:rwkv_proj], lpad], axis=1).astype(BF16),
            mu_m=_perm3(mu[:3 * width], width, heads).reshape(1, -1),
            mu_l=jnp.concatenate([mu[3 * width:], jnp.zeros((LORA_PAD - n_lora,), F32)]).reshape(1, -1),
            w0=_perm_cols(rwkv_w0[i], heads).reshape(1, -1),
            w2=_perm_cols(rwkv_w2[i], heads).astype(BF16),
            a0=_perm_cols(rwkv_a0[i], heads).reshape(1, -1),
            a2=_perm_cols(a2p, heads).astype(BF16),
            g2=_perm_cols(g2p, heads).astype(BF16),
            k_k=_perm_cols(rwkv_k_k[i], heads).reshape(1, -1),
            k_a=_perm_cols(rwkv_k_a[i], heads).reshape(1, -1),
            r_k=_perm_cols(rwkv_r_k[i].reshape(-1), heads).reshape(1, -1),
            ln_w=_perm_cols(rwkv_ln_w[i], heads).reshape(1, -1),
            ln_b=_perm_cols(rwkv_ln_b[i], heads).reshape(1, -1),
            w_out_r=wo[:width].astype(BF16).reshape(heads, RWKV_HEAD, -1).swapaxes(0, 1)
                    .reshape(width, -1),
            w_out_a=wo[width:].astype(BF16),
            w_up=w_up[i].astype(BF16), w_down=w_down[i].astype(BF16),
            w_ple_gate=w_ple_gate[i].astype(BF16), w_ple_proj=w_ple_proj[i].astype(BF16),
        )
        bias, gain = sb_bias[i], sb_norm[i]

        attend_p = lambda qkv: _sb_prompt(qkv, 0, bias, gain, heads=sb_heads)
        zero_shift = _shift_in(jnp.zeros((1, rwkv_proj), F32), width, heads)
        zero_wkv = jnp.zeros((1, heads, RWKV_HEAD, RWKV_HEAD), F32)
        yp, sp, wp, kp, vp = _layer(yp, p_prompt[i, 0], zero_shift, zero_wkv, attend_p, wts,
                                    seq_shift=True)

        n_phys = cache_k.shape[1]
        ck = cache_k.reshape((depth * n_phys,) + cache_k.shape[2:])
        cv = cache_v.reshape((depth * n_phys,) + cache_v.shape[2:])
        pages_i = page_table + i * n_phys
        attend_s = lambda qkv: _sb_sample(
            qkv[:, :sb_width].reshape(db, sb_heads, SB_HEAD), ck, cv, pages_i, bias, gain,
            heads=sb_heads).reshape(db, sb_width)
        ys, ss, ws, kn, vn = _layer(ys, p_sample[i, :, 0], _shift_in(state_rwkv_shift[i], width, heads),
                                    state_rwkv_wkv[i], attend_s, wts, seq_shift=False)
        for lst, val in zip(res, (sp, ss, wp, ws, kp[None], vp[None], kn[:, None], vn[:, None])):
            lst.append(val)
    return (yp[None], ys[:, None]) + tuple(jnp.stack(r) for r in res)
```

```python
import functools

import jax
import jax.numpy as jnp
import numpy as np
from jax import lax
from jax.experimental import pallas as pl
from jax.experimental.pallas import tpu as pltpu

F32 = jnp.float32
BF16 = jnp.bfloat16

RMS_EPS = 1e-6
GN_EPS = 64e-5
L2_EPS = 1e-12

LANES = 128
RWKV_HEAD = 64
SB_HEAD = 128
DECAY_LORA = 128
AAA_LORA = 96
GATE_LORA = 256
LORA_PAD = 512
VMEM_LIMIT = 52 * 1024 * 1024


def _cparams(sem, vmem=VMEM_LIMIT):
    return pltpu.CompilerParams(dimension_semantics=sem, vmem_limit_bytes=vmem)


def _tile(n, pref):
    if n <= pref:
        return n
    t = pref
    while n % t:
        t //= 2
    return t


def _mm_body(a_ref, b_ref, o_ref, acc_ref, *, nk, relu2):
    def finish(acc):
        if relu2:
            acc = jnp.square(jnp.maximum(acc, 0.0))
        o_ref[...] = acc.astype(o_ref.dtype)

    if nk == 1:
        finish(jnp.dot(a_ref[...], b_ref[...], preferred_element_type=F32))
    else:
        k = pl.program_id(2)

        @pl.when(k == 0)
        def _():
            acc_ref[...] = jnp.zeros_like(acc_ref)

        acc_ref[...] += jnp.dot(a_ref[...], b_ref[...], preferred_element_type=F32)

        @pl.when(k == nk - 1)
        def _():
            finish(acc_ref[...])


def _mm_vmem_bytes(tm, tn, tk, nk, out_dtype):
    out_b = tm * tn * jnp.dtype(out_dtype).itemsize
    acc_b = 3 * tm * tn * 4 if nk > 1 else 0
    return 2 * (2 * tm * tk + 2 * tk * tn + out_b) + acc_b + 2 * 1024 * 1024


def _matmul(a, b, *, out_dtype=F32, relu2=False, tm=1024, tn=1024, tk=2048, name="mm"):
    m, kdim = a.shape
    n = b.shape[1]
    tm, tn = _tile(m, tm), _tile(n, tn)
    tk = kdim if kdim <= 4096 else _tile(kdim, tk)
    nk = kdim // tk
    return pl.pallas_call(
        functools.partial(_mm_body, nk=nk, relu2=relu2),
        grid=(n // tn, m // tm, nk),
        in_specs=[pl.BlockSpec((tm, tk), lambda j, i, k: (i, k)),
                  pl.BlockSpec((tk, tn), lambda j, i, k: (k, j))],
        out_specs=pl.BlockSpec((tm, tn), lambda j, i, k: (i, j)),
        out_shape=jax.ShapeDtypeStruct((m, n), out_dtype),
        scratch_shapes=[pltpu.VMEM((tm, tn) if nk > 1 else (8, LANES), F32)],
        compiler_params=_cparams(("parallel", "parallel", "arbitrary"),
                                 max(VMEM_LIMIT, _mm_vmem_bytes(tm, tn, tk, nk, out_dtype))),
        name=name,
    )(a, b)


def _mm2_body(a1_ref, a2_ref, b1_ref, b2_ref, o_ref):
    o_ref[...] = (jnp.dot(a1_ref[...], b1_ref[...], preferred_element_type=F32)
                  + jnp.dot(a2_ref[...], b2_ref[...], preferred_element_type=F32))


def _matmul_cat(a1, a2, b1, b2, *, tm=1024, tn=1024):
    m, k1 = a1.shape
    k2 = a2.shape[1]
    n = b1.shape[1]
    tm, tn = _tile(m, tm), _tile(n, tn)
    return pl.pallas_call(
        _mm2_body,
        grid=(n // tn, m // tm),
        in_specs=[pl.BlockSpec((tm, k1), lambda j, i: (i, 0)),
                  pl.BlockSpec((tm, k2), lambda j, i: (i, 0)),
                  pl.BlockSpec((k1, tn), lambda j, i: (0, j)),
                  pl.BlockSpec((k2, tn), lambda j, i: (0, j))],
        out_specs=pl.BlockSpec((tm, tn), lambda j, i: (i, j)),
        out_shape=jax.ShapeDtypeStruct((m, n), F32),
        compiler_params=_cparams(("parallel", "parallel")),
        name="mm_out",
    )(a1, a2, b1, b2)


def _ple_body(a_ref, b_ref, x_ref, pe_ref, wp_ref, o_ref):
    gate = jnp.dot(a_ref[...], b_ref[...], preferred_element_type=F32)
    proj = jnp.dot(pe_ref[...], wp_ref[...], preferred_element_type=F32)
    o_ref[...] = x_ref[...] + jax.nn.sigmoid(gate) * proj


def _ple(xb, wg, x, pe, wp, *, tm=512, tn=1024):
    m, kdim = xb.shape
    n = wg.shape[1]
    kp = pe.shape[1]
    tm, tn = _tile(m, tm), _tile(n, tn)
    return pl.pallas_call(
        _ple_body,
        grid=(n // tn, m // tm),
        in_specs=[pl.BlockSpec((tm, kdim), lambda j, i: (i, 0)),
                  pl.BlockSpec((kdim, tn), lambda j, i: (0, j)),
                  pl.BlockSpec((tm, tn), lambda j, i: (i, j)),
                  pl.BlockSpec((tm, kp), lambda j, i: (i, 0)),
                  pl.BlockSpec((kp, tn), lambda j, i: (0, j))],
        out_specs=pl.BlockSpec((tm, tn), lambda j, i: (i, j)),
        out_shape=jax.ShapeDtypeStruct((m, n), F32),
        compiler_params=_cparams(("parallel", "parallel")),
        name="ple",
    )(xb, wg, x, pe, wp)


def _rms(x, g):
    return x * lax.rsqrt(jnp.mean(x * x, axis=-1, keepdims=True) + RMS_EPS) * g


def _norm_cast_body(x_ref, g_ref, o_ref):
    o_ref[...] = _rms(x_ref[...], g_ref[...]).astype(o_ref.dtype)


def _norm_cast(x, g, *, tm=256):
    m, d = x.shape
    tm = _tile(m, tm)
    return pl.pallas_call(
        _norm_cast_body,
        grid=(m // tm,),
        in_specs=[pl.BlockSpec((tm, d), lambda i: (i, 0)),
                  pl.BlockSpec((1, d), lambda i: (0, 0))],
        out_specs=pl.BlockSpec((tm, d), lambda i: (i, 0)),
        out_shape=jax.ShapeDtypeStruct((m, d), BF16),
        compiler_params=_cparams(("parallel",)),
        name="norm_cast",
    )(x, g.reshape(1, d))


def _resid_body(x_ref, f_ref, g_ref, gn_ref, xo_ref, no_ref, *, norm_next):
    xn = x_ref[...] + _rms(f_ref[...], g_ref[...])
    xo_ref[...] = xn
    if norm_next:
        no_ref[...] = _rms(xn, gn_ref[...]).astype(no_ref.dtype)
    else:
        no_ref[...] = xn.astype(no_ref.dtype)


def _resid_norm(x, f, g, g_next, *, tm=256):
    m, d = x.shape
    tm = _tile(m, tm)
    norm_next = g_next is not None
    gn = g_next if norm_next else g
    row = pl.BlockSpec((tm, d), lambda i: (i, 0))
    vec = pl.BlockSpec((1, d), lambda i: (0, 0))
    return pl.pallas_call(
        functools.partial(_resid_body, norm_next=norm_next),
        grid=(m // tm,),
        in_specs=[row, row, vec, vec],
        out_specs=[row, row],
        out_shape=[jax.ShapeDtypeStruct((m, d), F32), jax.ShapeDtypeStruct((m, d), BF16)],
        compiler_params=_cparams(("parallel",)),
        name="resid_norm",
    )(x, f, g.reshape(1, d), gn.reshape(1, d))


def _head_allsum(p, heads):
    q = p + pltpu.roll(p, 2 * heads, axis=1)
    return q + pltpu.roll(q, heads, axis=1)


def _col_slices(x):
    return [x[:, i * LANES:(i + 1) * LANES] for i in range(x.shape[1] // LANES)]


def _split2(x):
    hi = x.astype(BF16)
    return hi, (x - hi.astype(F32)).astype(BF16)


def _group_select(heads):
    src = np.arange(LANES)[:, None]
    dst = np.arange(2 * LANES)[None, :]
    mats = [(src == (2 * p + dst // LANES) * heads + dst % heads) for p in range(2)]
    return jnp.asarray(np.stack(mats), BF16)


def _prep_body(fm_ref, fl_ref, pm_ref, plo_ref, spm_ref, spl_ref, mum_ref, mul_ref, w0_ref, w2_ref,
               a0_ref, a2_ref, g2_ref, kk_ref, ka_ref, rk_ref, sel_ref,
               e_nkk, e_d, e_b, e_k, e_r, v_out, bonus_out, g_out, *, seq_shift, heads):
    width = heads * RWKV_HEAD

    def shifted(x, p_ref, sp_ref):
        if not seq_shift:
            return p_ref[...]
        before = jnp.where(pl.program_id(0) == 0, sp_ref[...], p_ref[7:8, :])
        row = lax.broadcasted_iota(jnp.int32, x.shape, 0)
        return jnp.where(row == 0, before, pltpu.roll(x, 1, axis=0))

    x = fm_ref[...]
    xm = x + (shifted(x, pm_ref, spm_ref) - x) * mum_ref[...]
    lo = fl_ref[...]
    lm = lo + (shifted(lo, plo_ref, spl_ref) - lo) * mul_ref[...]

    r = xm[:, :width]
    k = xm[:, width:2 * width]
    v = xm[:, 2 * width:]

    u = w0_ref[...] + jnp.dot(jnp.tanh(lm[:, :DECAY_LORA]).astype(BF16), w2_ref[...],
                              preferred_element_type=F32)
    w_log = -(jnp.maximum(-u, 0.0) + jnp.log1p(jnp.exp(-jnp.abs(u)))) - 0.5
    decay = jnp.exp(-jnp.exp(w_log))
    a = jax.nn.sigmoid(a0_ref[...] + jnp.dot(lm[:, DECAY_LORA:2 * DECAY_LORA].astype(BF16),
                                             a2_ref[...], preferred_element_type=F32))
    g = jnp.dot(jax.nn.sigmoid(lm[:, DECAY_LORA:]).astype(BF16), g2_ref[...],
                preferred_element_type=F32)

    kk = k * kk_ref[...]
    sq = _col_slices(kk * kk)
    tot = sq[0]
    for s in sq[1:]:
        tot = tot + s
    den = jnp.maximum(jnp.sqrt(_head_allsum(tot, heads)), L2_EPS)
    kk = jnp.concatenate([c / den for c in _col_slices(kk)], axis=1)
    kf = k * (1.0 + (a - 1.0) * ka_ref[...])
    b = kk * a

    rk_parts = _col_slices(r * kf * rk_ref[...])
    rk_tot = rk_parts[0]
    for s in rk_parts[1:]:
        rk_tot = rk_tot + s
    rk_tot = _head_allsum(rk_tot, heads)
    v_out[...] = v
    bonus_out[...] = jnp.concatenate([rk_tot * c for c in _col_slices(v)], axis=1)
    g_out[...] = g

    rows = x.shape[0]

    def expand(val, ref):
        hi, lo = _split2(jnp.concatenate(_col_slices(val), axis=0))
        for p in range(2):
            sel = sel_ref[p]
            out = (jnp.dot(hi, sel, preferred_element_type=F32)
                   + jnp.dot(lo, sel, preferred_element_type=F32))
            for i in range(width // LANES):
                for q in range(2):
                    ref[4 * i + 2 * p + q] = out[i * rows:(i + 1) * rows, q * LANES:(q + 1) * LANES]

    expand(-kk, e_nkk)
    expand(decay, e_d)
    expand(b, e_b)
    expand(kf, e_k)
    expand(r, e_r)


def _rwkv_prep(feats, lora, shift_m, shift_l, prm, *, seq_shift, heads, tm):
    m = feats.shape[0]
    width = heads * RWKV_HEAD
    tm = _tile(m, tm)
    nt = m // tm
    if seq_shift:
        assert tm % 8 == 0
        prev_m, prev_l = feats, lora
        pm_spec = pl.BlockSpec((8, 3 * width), lambda i: (jnp.maximum(i * (tm // 8) - 1, 0), 0))
        pl_spec = pl.BlockSpec((8, LORA_PAD), lambda i: (jnp.maximum(i * (tm // 8) - 1, 0), 0))
    else:
        prev_m, prev_l = shift_m, shift_l
        pm_spec = pl.BlockSpec((tm, 3 * width), lambda i: (i, 0))
        pl_spec = pl.BlockSpec((tm, LORA_PAD), lambda i: (i, 0))

    def vec(n):
        return pl.BlockSpec((1, n), lambda i: (0, 0))

    def full(r_, c_):
        return pl.BlockSpec((r_, c_), lambda i: (0, 0))

    e_spec = pl.BlockSpec((RWKV_HEAD, tm, LANES), lambda i: (0, i, 0))
    c_spec = pl.BlockSpec((tm, width), lambda i: (i, 0))
    e_shape = jax.ShapeDtypeStruct((RWKV_HEAD, m, LANES), F32)
    c_shape = jax.ShapeDtypeStruct((m, width), F32)
    return pl.pallas_call(
        functools.partial(_prep_body, seq_shift=seq_shift, heads=heads),
        grid=(nt,),
        in_specs=[pl.BlockSpec((tm, 3 * width), lambda i: (i, 0)),
                  pl.BlockSpec((tm, LORA_PAD), lambda i: (i, 0)),
                  pm_spec, pl_spec, vec(3 * width), vec(LORA_PAD), vec(3 * width), vec(LORA_PAD),
                  vec(width), full(DECAY_LORA, width), vec(width), full(DECAY_LORA, width),
                  full(LORA_PAD - DECAY_LORA, width), vec(width), vec(width), vec(width),
                  pl.BlockSpec((2, LANES, 2 * LANES), lambda i: (0, 0, 0))],
        out_specs=[e_spec] * 5 + [c_spec] * 3,
        out_shape=[e_shape] * 5 + [c_shape] * 3,
        compiler_params=_cparams(("parallel",)),
        name="rwkv_prep",
    )(feats, lora, prev_m, prev_l, shift_m[:1], shift_l[:1], prm["mu_m"], prm["mu_l"], prm["w0"],
      prm["w2"], prm["a0"], prm["a2"], prm["g2"], prm["k_k"], prm["k_a"], prm["r_k"],
      _group_select(heads))


N_ACC = 4


def _tree_sum(parts):
    while len(parts) > 1:
        parts = [parts[i] + parts[i + 1] for i in range(0, len(parts), 2)]
    return parts[0]


def _wkv_sa(s_ref, nkk_ref, t):
    acc = [None] * N_ACC
    for kx in range(RWKV_HEAD):
        term = s_ref[kx] * nkk_ref[kx, pl.ds(t, 1), :]
        acc[kx % N_ACC] = term if acc[kx % N_ACC] is None else acc[kx % N_ACC] + term
    return _tree_sum(acc)


def _wkv_step(s_ref, nkk_ref, d_ref, b_ref, k_ref, r_ref, v_t, sa, t, t_next):
    yacc = [None] * N_ACC
    sacc = [None] * N_ACC
    for kx in range(RWKV_HEAD):
        s_new = (s_ref[kx] * d_ref[kx, pl.ds(t, 1), :] + sa * b_ref[kx, pl.ds(t, 1), :]
                 + v_t * k_ref[kx, pl.ds(t, 1), :])
        s_ref[kx] = s_new
        term = s_new * r_ref[kx, pl.ds(t, 1), :]
        yacc[kx % N_ACC] = term if yacc[kx % N_ACC] is None else yacc[kx % N_ACC] + term
        if t_next is not None:
            term = s_new * nkk_ref[kx, pl.ds(t_next, 1), :]
            sacc[kx % N_ACC] = term if sacc[kx % N_ACC] is None else sacc[kx % N_ACC] + term
    return _tree_sum(yacc), (None if t_next is None else _tree_sum(sacc))


def _scan_seq_body(s0_ref, nkk_ref, d_ref, b_ref, k_ref, r_ref, v_ref, y_ref, sout_ref, s_ref, *,
                   tc):
    c = pl.program_id(0)

    @pl.when(c == 0)
    def _():
        s_ref[...] = s0_ref[...]

    def step(t, sa):
        y_ref[t], sa_next = _wkv_step(s_ref, nkk_ref, d_ref, b_ref, k_ref, r_ref, v_ref[t], sa, t,
                                      jnp.minimum(t + 1, tc - 1))
        return sa_next

    lax.fori_loop(0, tc, step, _wkv_sa(s_ref, nkk_ref, 0), unroll=16)

    @pl.when(c == pl.num_programs(0) - 1)
    def _():
        sout_ref[...] = s_ref[...]


def _wkv_scan_seq(s0, e_list, v, *, tc=64):
    t_len = v.shape[0]
    tc = _tile(t_len, tc)
    s_spec = pl.BlockSpec(s0.shape, lambda c: (0, 0, 0))
    e_spec = pl.BlockSpec((RWKV_HEAD, tc, LANES), lambda c: (0, c, 0))
    v_spec = pl.BlockSpec((tc,) + v.shape[1:], lambda c: (c, 0, 0))
    return pl.pallas_call(
        functools.partial(_scan_seq_body, tc=tc),
        grid=(t_len // tc,),
        in_specs=[s_spec] + [e_spec] * 5 + [v_spec],
        out_specs=[v_spec, s_spec],
        out_shape=[jax.ShapeDtypeStruct(v.shape, F32), jax.ShapeDtypeStruct(s0.shape, F32)],
        scratch_shapes=[pltpu.VMEM(s0.shape, F32)],
        compiler_params=_cparams(("arbitrary",)),
        name="wkv_scan",
    )(s0, *e_list, v)


def _scan_batch_body(s0_ref, nkk_ref, d_ref, b_ref, k_ref, r_ref, v_ref, y_ref, sout_ref):
    bidx = pl.program_id(0)
    s = sout_ref.at[0]
    s[...] = s0_ref[0]
    y_ref[0], _ = _wkv_step(s, nkk_ref, d_ref, b_ref, k_ref, r_ref, v_ref[0],
                            _wkv_sa(s, nkk_ref, bidx), bidx, None)


def _wkv_scan_batch(s0, e_list, v):
    nb = v.shape[0]
    s_spec = pl.BlockSpec((1,) + s0.shape[1:], lambda b: (b, 0, 0, 0))
    e_spec = pl.BlockSpec((RWKV_HEAD, nb, LANES), lambda b: (0, 0, 0))
    v_spec = pl.BlockSpec((1,) + v.shape[1:], lambda b: (b, 0, 0))
    return pl.pallas_call(
        _scan_batch_body,
        grid=(nb,),
        in_specs=[s_spec] + [e_spec] * 5 + [v_spec],
        out_specs=[v_spec, s_spec],
        out_shape=[jax.ShapeDtypeStruct(v.shape, F32), jax.ShapeDtypeStruct(s0.shape, F32)],
        compiler_params=_cparams(("arbitrary",)),
        name="wkv_step",
    )(s0, *e_list, v)


def _post_body(y_ref, bonus_ref, g_ref, lnw_ref, lnb_ref, o_ref, *, heads):
    inv_n = 1.0 / RWKV_HEAD

    def head_sum(x):
        cols = _col_slices(x)
        tot = cols[0]
        for c in cols[1:]:
            tot = tot + c
        return _head_allsum(tot, heads)

    def tiled(stat, like):
        return jnp.concatenate([stat] * (like.shape[1] // LANES), axis=1)

    y = y_ref[...]
    yc = y - tiled(head_sum(y) * inv_n, y)
    var = head_sum(yc * yc) * inv_n
    yn = yc * tiled(lax.rsqrt(var + GN_EPS), y) * lnw_ref[...] + lnb_ref[...]
    o_ref[...] = ((yn + bonus_ref[...]) * g_ref[...]).astype(o_ref.dtype)


def _rwkv_post(y, bonus, g, prm, *, heads, tm=256):
    m, width = y.shape
    tm = _tile(m, tm)
    row = pl.BlockSpec((tm, width), lambda i: (i, 0))
    vec = pl.BlockSpec((1, width), lambda i: (0, 0))
    return pl.pallas_call(
        functools.partial(_post_body, heads=heads),
        grid=(m // tm,),
        in_specs=[row] * 3 + [vec] * 2,
        out_specs=row,
        out_shape=jax.ShapeDtypeStruct((m, width), BF16),
        compiler_params=_cparams(("parallel",)),
        name="rwkv_post",
    )(y, bonus, g, prm["ln_w"], prm["ln_b"])


SB_TILE = 512
SB_SUB = 256
SB_HEADS_PER_STEP = 8
LOG2E = 1.4426950408889634


def _log_sigmoid_pair(z):
    lsz = jnp.minimum(z, 0.0) - jnp.log(1.0 + jnp.exp2(jnp.abs(z) * (-LOG2E)))
    return lsz, lsz - z


def _sb_weights(lsz, lsn, n_blocks, tri2_ref, carry, valid):
    rows = lsz.shape[0] // n_blocks
    hi = lsn.astype(BF16)
    lo = (lsn - hi.astype(F32)).astype(BF16)
    if n_blocks == 1:
        tri = tri2_ref[:lsz.shape[1]]
        logw = lsz + (jnp.dot(hi, tri, preferred_element_type=F32)
                      + jnp.dot(lo, tri, preferred_element_type=F32))
    else:
        logw = lsz + jnp.dot(jnp.concatenate([hi, lo], axis=1), tri2_ref[...],
                             preferred_element_type=F32)
    taken = jnp.sum(lsn, axis=1, keepdims=True)
    weights = []
    for b in range(n_blocks):
        sl = slice(b * rows, (b + 1) * rows)
        w = jnp.exp(logw[sl] + carry)
        if valid is not None:
            w = jnp.where(valid[sl], w, 0.0)
        weights.append(w.astype(BF16))
        carry = carry + taken[sl]
    return weights, carry


def _tri2(n):
    tri = (lax.broadcasted_iota(jnp.int32, (n, n), 0)
           > lax.broadcasted_iota(jnp.int32, (n, n), 1)).astype(BF16)
    return jnp.concatenate([tri, tri], axis=0)


def _suffix_sum(ls, tri_ref):
    hi = ls.astype(BF16)
    lo = (ls - hi.astype(F32)).astype(BF16)
    tri = tri_ref[...]
    return jnp.dot(jnp.concatenate([hi, lo], axis=1), jnp.concatenate([tri, tri], axis=0),
                   preferred_element_type=F32)


def _sbp_body(qi_ref, kb_ref, q_ref, k_ref, v_ref, tri_ref, g_ref, bias_ref, ones_ref,
              o_ref, qs_ref, acc_ref, carry_ref):
    s = pl.program_id(1)
    tile = q_ref.shape[0]
    n_sub = tri_ref.shape[0]
    n_heads = q_ref.shape[1] // SB_HEAD
    diagonal = kb_ref[s] == qi_ref[s]

    def visit(masked):
        if masked:
            causal = (lax.broadcasted_iota(jnp.int32, (tile, tile), 1)
                      < lax.broadcasted_iota(jnp.int32, (tile, tile), 0))
        for hh in range(n_heads):
            cols = slice(hh * SB_HEAD, (hh + 1) * SB_HEAD)
            kmat = jnp.concatenate([k_ref[:, cols].astype(BF16), ones_ref[...]], axis=1)
            vmat = v_ref[:, cols].astype(BF16)
            z = lax.dot_general(qs_ref[:, 2 * hh * SB_HEAD:2 * (hh + 1) * SB_HEAD], kmat,
                                (((1,), (1,)), ((), ())), preferred_element_type=F32)
            lsz, lsn = _log_sigmoid_pair(z)
            if masked:
                lsn = jnp.where(causal, lsn, 0.0)
            carry = carry_ref[hh]
            w_parts = [None] * (tile // n_sub)
            for sub in reversed(range(tile // n_sub)):
                sl = slice(sub * n_sub, (sub + 1) * n_sub)
                w = jnp.exp(lsz[:, sl] + _suffix_sum(lsn[:, sl], tri_ref) + carry)
                if masked:
                    w = jnp.where(causal[:, sl], w, 0.0)
                w_parts[sub] = w.astype(BF16)
                carry = carry + jnp.sum(lsn[:, sl], axis=1, keepdims=True)
            carry_ref[hh] = carry
            acc_ref[:, cols] += jnp.dot(jnp.concatenate(w_parts, axis=1), vmat,
                                        preferred_element_type=F32)

    @pl.when(diagonal)
    def _():
        for hh in range(n_heads):
            cols = slice(hh * SB_HEAD, (hh + 1) * SB_HEAD)
            qs_ref[:, 2 * hh * SB_HEAD:(2 * hh + 1) * SB_HEAD] = (
                q_ref[:, cols] * (SB_HEAD ** -0.5)).astype(BF16)
            qs_ref[:, (2 * hh + 1) * SB_HEAD:2 * (hh + 1) * SB_HEAD] = jnp.broadcast_to(
                bias_ref[hh], (tile, SB_HEAD))
        acc_ref[...] = jnp.zeros_like(acc_ref)
        carry_ref[...] = jnp.zeros_like(carry_ref)
        visit(True)

    @pl.when(jnp.logical_not(diagonal))
    def _():
        visit(False)

    @pl.when(kb_ref[s] == 0)
    def _():
        for hh in range(n_heads):
            cols = slice(hh * SB_HEAD, (hh + 1) * SB_HEAD)
            o_ref[:, cols] = _rms(acc_ref[:, cols], g_ref[hh]).astype(o_ref.dtype)


def _sb_schedule(n_tiles):
    qi = [i for i in range(n_tiles) for _ in range(i + 1)]
    kb = [j for i in range(n_tiles) for j in range(i, -1, -1)]
    return [jnp.asarray(np.asarray(a, np.int32)) for a in (qi, kb)]


def _sb_prompt(qkv, q_col, bias, gain, *, heads):
    t_len = qkv.shape[0]
    tile = _tile(t_len, SB_TILE)
    sched = _sb_schedule(t_len // tile)
    nsteps = sched[0].shape[0]
    sub = min(SB_SUB, tile)
    hps = SB_HEADS_PER_STEP
    assert heads % hps == 0 and q_col % hps == 0
    wide = hps * SB_HEAD
    qc = q_col // hps
    b_hi = bias.astype(BF16)
    b_lo = (bias - b_hi.astype(F32)).astype(BF16)
    lane = lax.broadcasted_iota(jnp.int32, (heads, 1, SB_HEAD), 2)
    bias_ext = jnp.where(lane == 0, b_hi[:, None, None],
                         jnp.where(lane == 1, b_lo[:, None, None], jnp.zeros((), BF16)))
    ones_ext = (lax.broadcasted_iota(jnp.int32, (tile, SB_HEAD), 1) < 2).astype(BF16)
    grid_spec = pltpu.PrefetchScalarGridSpec(
        num_scalar_prefetch=2,
        grid=(heads // hps, nsteps),
        in_specs=[
            pl.BlockSpec((tile, wide), lambda h, s, qi, kb: (qi[s], qc + h)),
            pl.BlockSpec((tile, wide), lambda h, s, qi, kb: (kb[s], qc + heads // hps + h)),
            pl.BlockSpec((tile, wide), lambda h, s, qi, kb: (kb[s], qc + 2 * (heads // hps) + h)),
            pl.BlockSpec((sub, sub), lambda h, s, qi, kb: (0, 0)),
            pl.BlockSpec((hps, 1, SB_HEAD), lambda h, s, qi, kb: (h, 0, 0)),
            pl.BlockSpec((hps, 1, SB_HEAD), lambda h, s, qi, kb: (h, 0, 0)),
            pl.BlockSpec((tile, SB_HEAD), lambda h, s, qi, kb: (0, 0)),
        ],
        out_specs=pl.BlockSpec((tile, wide), lambda h, s, qi, kb: (qi[s], h)),
        scratch_shapes=[pltpu.VMEM((tile, 2 * wide), BF16), pltpu.VMEM((tile, wide), F32),
                        pltpu.VMEM((hps, tile, 1), F32)],
    )
    return pl.pallas_call(
        _sbp_body,
        grid_spec=grid_spec,
        out_shape=jax.ShapeDtypeStruct((t_len, heads * SB_HEAD), BF16),
        compiler_params=_cparams(("parallel", "arbitrary")),
        name="sb_prompt",
    )(*sched, qkv, qkv, qkv, _tri2(sub)[:sub], gain.reshape(heads, 1, SB_HEAD), bias_ext, ones_ext)


SBS_PAGES = 8


def _sbs_body(pt_ref, q_ref, *refs, heads, pages):
    k_refs, v_refs = refs[:pages], refs[pages:2 * pages]
    tri_ref, bias_ref, g_ref, o_ref, qs_ref, acc_ref, carry_ref = refs[2 * pages:]
    p = pl.program_id(1)
    rows = k_refs[0].shape[1] * heads
    n_sub = tri_ref.shape[1]
    subs = list(reversed(range(rows // n_sub)))

    @pl.when(p == 0)
    def _():
        qs_ref[...] = (q_ref[0] * (SB_HEAD ** -0.5)).astype(BF16)
        acc_ref[...] = jnp.zeros_like(acc_ref)
        carry_ref[...] = jnp.zeros_like(carry_ref)

    z2_blocks = []
    for k_ref in k_refs:
        kmat = k_ref[0].reshape(rows, SB_HEAD).astype(BF16)
        z2 = lax.dot_general(qs_ref[...], kmat, (((1,), (1,)), ((), ())),
                             preferred_element_type=F32) + bias_ref[...]
        z2_blocks += [z2[:, u * n_sub:(u + 1) * n_sub] for u in subs]
    stacked = (len(z2_blocks) * heads, n_sub)
    own_head = (lax.broadcasted_iota(jnp.int32, stacked, 1) % heads
                == lax.broadcasted_iota(jnp.int32, stacked, 0) % heads)
    lsz, lsn = _log_sigmoid_pair(jnp.concatenate(z2_blocks, axis=0))
    weights, carry_ref[...] = _sb_weights(lsz, jnp.where(own_head, lsn, 0.0), len(z2_blocks),
                                          tri_ref, carry_ref[...], own_head)
    acc = acc_ref[...]
    for u, v_ref in enumerate(v_refs):
        vmat = v_ref[0].reshape(rows, SB_HEAD).astype(BF16)
        w_page = weights[u * len(subs):(u + 1) * len(subs)][::-1]
        acc = acc + jnp.dot(jnp.concatenate(w_page, axis=1), vmat, preferred_element_type=F32)
    acc_ref[...] = acc

    @pl.when(p == pl.num_programs(1) - 1)
    def _():
        o_ref[0] = _rms(acc_ref[...], g_ref[...]).astype(o_ref.dtype)


def _sb_sample(q, cache_k, cache_v, page_table, bias, gain, *, heads):
    nb = q.shape[0]
    n_pages = page_table.shape[1]
    page = cache_k.shape[1]
    sub = min(SB_SUB, page * heads)
    pages = SBS_PAGES
    while n_pages % pages:
        pages //= 2

    def kv_spec(u):
        return pl.BlockSpec((1, page, heads, SB_HEAD),
                            lambda b, p, pt: (pt[b, n_pages - 1 - (p * pages + u)], 0, 0, 0))

    kv_specs = [kv_spec(u) for u in range(pages)]
    grid_spec = pltpu.PrefetchScalarGridSpec(
        num_scalar_prefetch=1,
        grid=(nb, n_pages // pages),
        in_specs=[pl.BlockSpec((1, heads, SB_HEAD), lambda b, p, pt: (b, 0, 0))]
                 + kv_specs + kv_specs
                 + [pl.BlockSpec((2 * sub, sub), lambda b, p, pt: (0, 0)),
                    pl.BlockSpec((heads, 1), lambda b, p, pt: (0, 0)),
                    pl.BlockSpec((heads, SB_HEAD), lambda b, p, pt: (0, 0))],
        out_specs=pl.BlockSpec((1, heads, SB_HEAD), lambda b, p, pt: (b, 0, 0)),
        scratch_shapes=[pltpu.VMEM((heads, SB_HEAD), BF16), pltpu.VMEM((heads, SB_HEAD), F32),
                        pltpu.VMEM((heads, 1), F32)],
    )
    return pl.pallas_call(
        functools.partial(_sbs_body, heads=heads, pages=pages),
        grid_spec=grid_spec,
        out_shape=jax.ShapeDtypeStruct((nb, heads, SB_HEAD), BF16),
        compiler_params=_cparams(("parallel", "arbitrary")),
        name="sb_sample",
    )(page_table, q, *([cache_k] * pages), *([cache_v] * pages), _tri2(sub),
      bias.reshape(heads, 1), gain)


def _perm_cols(w, heads):
    lead = w.shape[:-1]
    return w.reshape(lead + (heads, RWKV_HEAD)).swapaxes(-1, -2).reshape(lead + (heads * RWKV_HEAD,))


def _unperm_cols(w, heads):
    lead = w.shape[:-1]
    return w.reshape(lead + (RWKV_HEAD, heads)).swapaxes(-1, -2).reshape(lead + (heads * RWKV_HEAD,))


def _state_in(s):
    heads = s.shape[-3]
    lead = s.shape[:-3]
    s = jnp.moveaxis(s, -3, -1)
    s = jnp.swapaxes(s, -3, -2)
    return s.reshape(lead + (RWKV_HEAD, RWKV_HEAD * heads // LANES, LANES))


def _state_out(s, heads):
    lead = s.shape[:-3]
    s = s.reshape(lead + (RWKV_HEAD, RWKV_HEAD, heads))
    s = jnp.swapaxes(s, -3, -2)
    return jnp.moveaxis(s, -1, -3)


def _layer(x, pe, shift_prev, wkv0, attend, wts, *, seq_shift):
    heads = wts["heads"]
    width = heads * RWKV_HEAD
    sb_heads = wts["sb_heads"]
    sb_width = sb_heads * SB_HEAD
    m = x.shape[0]

    n_lora = DECAY_LORA + AAA_LORA + GATE_LORA
    h = _norm_cast(x, wts["norm_mix_pre"])
    feats = _matmul(h, wts["w_rkv"], name="mm_in")
    qkv = _matmul(h, wts["w_qkv"], name="mm_qkv")
    lora = _matmul(h, wts["w_lora"], name="mm_lora")

    rows = slice(m - 1, m) if seq_shift else slice(0, m)
    shift_new = jnp.concatenate(
        [_unperm_cols(feats[rows].reshape(-1, 3, width), heads).reshape(-1, 3 * width),
         lora[rows, :n_lora]], axis=1)
    tm = _tile(m, 64) if seq_shift else m
    outs = _rwkv_prep(feats, lora, shift_prev[:, :3 * width], shift_prev[:, 3 * width:], wts,
                      seq_shift=seq_shift, heads=heads, tm=tm)
    e_list, (v_c, bonus_c, g_c) = outs[:5], outs[5:]
    v_rows = v_c.reshape(m, width // LANES, LANES)
    if seq_shift:
        y, s_fin = _wkv_scan_seq(_state_in(wkv0[0]), e_list, v_rows)
        wkv_new = _state_out(s_fin, heads)[None]
    else:
        y, s_fin = _wkv_scan_batch(_state_in(wkv0), e_list, v_rows)
        wkv_new = _state_out(s_fin, heads)
    r_out = _rwkv_post(y.reshape(m, width), bonus_c, g_c, wts, heads=heads)

    a_out = attend(qkv)
    kh = qkv[:, sb_width:2 * sb_width].reshape(m, sb_heads, SB_HEAD)
    vh = qkv[:, 2 * sb_width:].reshape(m, sb_heads, SB_HEAD)

    mix = _matmul_cat(r_out, a_out, wts["w_out_r"], wts["w_out_a"])
    x1, h2 = _resid_norm(x, mix, wts["norm_mix_post"], wts["norm_ffn_pre"])
    up = _matmul(h2, wts["w_up"], out_dtype=BF16, relu2=True, name="mm_up")
    f = _matmul(up, wts["w_down"], tk=4096, name="mm_down")
    x2, x2b = _resid_norm(x1, f, wts["norm_ffn_post"], None)
    out = _ple(x2b, wts["w_ple_gate"], x2, pe.astype(BF16), wts["w_ple_proj"])
    return out, shift_new, wkv_new, kh, vh


def _perm3(w, width, heads):
    lead = w.shape[:-1]
    return _perm_cols(w.reshape(lead + (3, width)), heads).reshape(lead + (3 * width,))


def _shift_in(s, width, heads):
    pad = LORA_PAD - (s.shape[1] - 3 * width)
    return jnp.concatenate([_perm3(s[:, :3 * width], width, heads), s[:, 3 * width:],
                            jnp.zeros((s.shape[0], pad), s.dtype)], axis=1)


def kernel(x_prompt, x_sample, p_prompt, p_sample, state_rwkv_shift, state_rwkv_wkv, cache_k, cache_v, page_table, norm_mix_pre, norm_mix_post, norm_ffn_pre, norm_ffn_post, w_in, rwkv_mu, rwkv_w0, rwkv_w2, rwkv_a0, rwkv_a2, rwkv_g2, rwkv_k_k, rwkv_k_a, rwkv_r_k, rwkv_ln_w, rwkv_ln_b, sb_norm, sb_bias, w_out, w_up, w_down, w_ple_gate, w_ple_proj):
    depth = w_in.shape[0]
    heads = rwkv_r_k.shape[1]
    width = heads * RWKV_HEAD
    sb_heads = sb_bias.shape[1]
    sb_width = sb_heads * SB_HEAD
    nb, t_len, d_model = x_prompt.shape
    db = x_sample.shape[0]
    assert nb == 1 and x_sample.shape[1] == 1 and LANES == 4 * heads
    rwkv_proj = rwkv_mu.shape[1]
    n_lora = rwkv_proj - 3 * width
    assert n_lora == DECAY_LORA + AAA_LORA + GATE_LORA and sb_width + width == w_out.shape[1]

    yp, ys = x_prompt[0], x_sample[:, 0]
    res = [[] for _ in range(8)]
    for i in range(depth):
        wi = w_in[i]
        lpad = jnp.zeros((d_model, LORA_PAD - n_lora), wi.dtype)
        mu = rwkv_mu[i]
        a2p = jnp.concatenate([rwkv_a2[i], jnp.zeros((DECAY_LORA - AAA_LORA, width), F32)], axis=0)
        g2p = jnp.concatenate([jnp.zeros((AAA_LORA, width), F32), rwkv_g2[i],
                               jnp.zeros((LORA_PAD - n_lora, width), F32)], axis=0)
        wo = w_out[i]
        wts = dict(
            heads=heads, sb_heads=sb_heads,
            norm_mix_pre=norm_mix_pre[i], norm_mix_post=norm_mix_post[i],
            norm_ffn_pre=norm_ffn_pre[i], norm_ffn_post=norm_ffn_post[i],
            w_rkv=_perm3(wi[:, :3 * width], width, heads).astype(BF16),
            w_qkv=wi[:, rwkv_proj:].astype(BF16),
            w_lora=jnp.concatenate([wi[:, 3 * width:rwkv_proj], lpad], axis=1).astype(BF16),
            mu_m=_perm3(mu[:3 * width], width, heads).reshape(1, -1),
            mu_l=jnp.concatenate([mu[3 * width:], jnp.zeros((LORA_PAD - n_lora,), F32)]).reshape(1, -1),
            w0=_perm_cols(rwkv_w0[i], heads).reshape(1, -1),
            w2=_perm_cols(rwkv_w2[i], heads).astype(BF16),
            a0=_perm_cols(rwkv_a0[i], heads).reshape(1, -1),
            a2=_perm_cols(a2p, heads).astype(BF16),
            g2=_perm_cols(g2p, heads).astype(BF16),
            k_k=_perm_cols(rwkv_k_k[i], heads).reshape(1, -1),
            k_a=_perm_cols(rwkv_k_a[i], heads).reshape(1, -1),
            r_k=_perm_cols(rwkv_r_k[i].reshape(-1), heads).reshape(1, -1),
            ln_w=_perm_cols(rwkv_ln_w[i], heads).reshape(1, -1),
            ln_b=_perm_cols(rwkv_ln_b[i], heads).reshape(1, -1),
            w_out_r=wo[:width].astype(BF16).reshape(heads, RWKV_HEAD, -1).swapaxes(0, 1)
                    .reshape(width, -1),
            w_out_a=wo[width:].astype(BF16),
            w_up=w_up[i].astype(BF16), w_down=w_down[i].astype(BF16),
            w_ple_gate=w_ple_gate[i].astype(BF16), w_ple_proj=w_ple_proj[i].astype(BF16),
        )
        bias, gain = sb_bias[i], sb_norm[i]

        attend_p = lambda qkv: _sb_prompt(qkv, 0, bias, gain, heads=sb_heads)
        zero_shift = _shift_in(jnp.zeros((1, rwkv_proj), F32), width, heads)
        zero_wkv = jnp.zeros((1, heads, RWKV_HEAD, RWKV_HEAD), F32)
        yp, sp, wp, kp, vp = _layer(yp, p_prompt[i, 0], zero_shift, zero_wkv, attend_p, wts,
                                    seq_shift=True)

        n_phys = cache_k.shape[1]
        ck = cache_k.reshape((depth * n_phys,) + cache_k.shape[2:])
        cv = cache_v.reshape((depth * n_phys,) + cache_v.shape[2:])
        pages_i = page_table + i * n_phys
        attend_s = lambda qkv: _sb_sample(
            qkv[:, :sb_width].reshape(db, sb_heads, SB_HEAD), ck, cv, pages_i, bias, gain,
            heads=sb_heads).reshape(db, sb_width)
        ys, ss, ws, kn, vn = _layer(ys, p_sample[i, :, 0], _shift_in(state_rwkv_shift[i], width, heads),
                                    state_rwkv_wkv[i], attend_s, wts, seq_shift=False)
        for lst, val in zip(res, (sp, ss, wp, ws, kp[None], vp[None], kn[:, None], vn[:, None])):
            lst.append(val)
    return (yp[None], ys[:, None]) + tuple(jnp.stack(r) for r in res)
```

```python
import functools

import jax
import jax.numpy as jnp
import numpy as np
from jax import lax
from jax.experimental import pallas as pl
from jax.experimental.pallas import tpu as pltpu

F32 = jnp.float32
BF16 = jnp.bfloat16

RMS_EPS = 1e-6
GN_EPS = 64e-5
L2_EPS = 1e-12

LANES = 128
RWKV_HEAD = 64
SB_HEAD = 128
DECAY_LORA = 128
AAA_LORA = 96
GATE_LORA = 256
LORA_PAD = 512
VMEM_LIMIT = 52 * 1024 * 1024


def _cparams(sem, vmem=VMEM_LIMIT):
    return pltpu.CompilerParams(dimension_semantics=sem, vmem_limit_bytes=vmem)


def _tile(n, pref):
    if n <= pref:
        return n
    t = pref
    while n % t:
        t //= 2
    return t


def _mm_body(a_ref, b_ref, o_ref, acc_ref, *, nk, relu2):
    def finish(acc):
        if relu2:
            acc = jnp.square(jnp.maximum(acc, 0.0))
        o_ref[...] = acc.astype(o_ref.dtype)

    if nk == 1:
        finish(jnp.dot(a_ref[...], b_ref[...], preferred_element_type=F32))
    else:
        k = pl.program_id(2)

        @pl.when(k == 0)
        def _():
            acc_ref[...] = jnp.zeros_like(acc_ref)

        acc_ref[...] += jnp.dot(a_ref[...], b_ref[...], preferred_element_type=F32)

        @pl.when(k == nk - 1)
        def _():
            finish(acc_ref[...])


def _mm_vmem_bytes(tm, tn, tk, nk, out_dtype):
    out_b = tm * tn * jnp.dtype(out_dtype).itemsize
    acc_b = 3 * tm * tn * 4 if nk > 1 else 0
    return 2 * (2 * tm * tk + 2 * tk * tn + out_b) + acc_b + 2 * 1024 * 1024


def _matmul(a, b, *, out_dtype=F32, relu2=False, tm=1024, tn=1024, tk=2048, name="mm"):
    m, kdim = a.shape
    n = b.shape[1]
    tm, tn = _tile(m, tm), _tile(n, tn)
    tk = kdim if kdim <= 4096 else _tile(kdim, tk)
    nk = kdim // tk
    return pl.pallas_call(
        functools.partial(_mm_body, nk=nk, relu2=relu2),
        grid=(n // tn, m // tm, nk),
        in_specs=[pl.BlockSpec((tm, tk), lambda j, i, k: (i, k)),
                  pl.BlockSpec((tk, tn), lambda j, i, k: (k, j))],
        out_specs=pl.BlockSpec((tm, tn), lambda j, i, k: (i, j)),
        out_shape=jax.ShapeDtypeStruct((m, n), out_dtype),
        scratch_shapes=[pltpu.VMEM((tm, tn) if nk > 1 else (8, LANES), F32)],
        compiler_params=_cparams(("parallel", "parallel", "arbitrary"),
                                 max(VMEM_LIMIT, _mm_vmem_bytes(tm, tn, tk, nk, out_dtype))),
        name=name,
    )(a, b)


def _mm2_body(a1_ref, a2_ref, b1_ref, b2_ref, o_ref):
    o_ref[...] = (jnp.dot(a1_ref[...], b1_ref[...], preferred_element_type=F32)
                  + jnp.dot(a2_ref[...], b2_ref[...], preferred_element_type=F32))


def _matmul_cat(a1, a2, b1, b2, *, tm=1024, tn=1024):
    m, k1 = a1.shape
    k2 = a2.shape[1]
    n = b1.shape[1]
    tm, tn = _tile(m, tm), _tile(n, tn)
    return pl.pallas_call(
        _mm2_body,
        grid=(n // tn, m // tm),
        in_specs=[pl.BlockSpec((tm, k1), lambda j, i: (i, 0)),
                  pl.BlockSpec((tm, k2), lambda j, i: (i, 0)),
                  pl.BlockSpec((k1, tn), lambda j, i: (0, j)),
                  pl.BlockSpec((k2, tn), lambda j, i: (0, j))],
        out_specs=pl.BlockSpec((tm, tn), lambda j, i: (i, j)),
        out_shape=jax.ShapeDtypeStruct((m, n), F32),
        compiler_params=_cparams(("parallel", "parallel")),
        name="mm_out",
    )(a1, a2, b1, b2)


def _ple_body(a_ref, b_ref, x_ref, pe_ref, wp_ref, o_ref):
    gate = jnp.dot(a_ref[...], b_ref[...], preferred_element_type=F32)
    proj = jnp.dot(pe_ref[...], wp_ref[...], preferred_element_type=F32)
    o_ref[...] = x_ref[...] + jax.nn.sigmoid(gate) * proj


def _ple(xb, wg, x, pe, wp, *, tm=512, tn=1024):
    m, kdim = xb.shape
    n = wg.shape[1]
    kp = pe.shape[1]
    tm, tn = _tile(m, tm), _tile(n, tn)
    return pl.pallas_call(
        _ple_body,
        grid=(n // tn, m // tm),
        in_specs=[pl.BlockSpec((tm, kdim), lambda j, i: (i, 0)),
                  pl.BlockSpec((kdim, tn), lambda j, i: (0, j)),
                  pl.BlockSpec((tm, tn), lambda j, i: (i, j)),
                  pl.BlockSpec((tm, kp), lambda j, i: (i, 0)),
                  pl.BlockSpec((kp, tn), lambda j, i: (0, j))],
        out_specs=pl.BlockSpec((tm, tn), lambda j, i: (i, j)),
        out_shape=jax.ShapeDtypeStruct((m, n), F32),
        compiler_params=_cparams(("parallel", "parallel")),
        name="ple",
    )(xb, wg, x, pe, wp)


def _rms(x, g):
    return x * lax.rsqrt(jnp.mean(x * x, axis=-1, keepdims=True) + RMS_EPS) * g


def _norm_cast_body(x_ref, g_ref, o_ref):
    o_ref[...] = _rms(x_ref[...], g_ref[...]).astype(o_ref.dtype)


def _norm_cast(x, g, *, tm=256):
    m, d = x.shape
    tm = _tile(m, tm)
    return pl.pallas_call(
        _norm_cast_body,
        grid=(m // tm,),
        in_specs=[pl.BlockSpec((tm, d), lambda i: (i, 0)),
                  pl.BlockSpec((1, d), lambda i: (0, 0))],
        out_specs=pl.BlockSpec((tm, d), lambda i: (i, 0)),
        out_shape=jax.ShapeDtypeStruct((m, d), BF16),
        compiler_params=_cparams(("parallel",)),
        name="norm_cast",
    )(x, g.reshape(1, d))


def _resid_body(x_ref, f_ref, g_ref, gn_ref, xo_ref, no_ref, *, norm_next):
    xn = x_ref[...] + _rms(f_ref[...], g_ref[...])
    xo_ref[...] = xn
    if norm_next:
        no_ref[...] = _rms(xn, gn_ref[...]).astype(no_ref.dtype)
    else:
        no_ref[...] = xn.astype(no_ref.dtype)


def _resid_norm(x, f, g, g_next, *, tm=256):
    m, d = x.shape
    tm = _tile(m, tm)
    norm_next = g_next is not None
    gn = g_next if norm_next else g
    row = pl.BlockSpec((tm, d), lambda i: (i, 0))
    vec = pl.BlockSpec((1, d), lambda i: (0, 0))
    return pl.pallas_call(
        functools.partial(_resid_body, norm_next=norm_next),
        grid=(m // tm,),
        in_specs=[row, row, vec, vec],
        out_specs=[row, row],
        out_shape=[jax.ShapeDtypeStruct((m, d), F32), jax.ShapeDtypeStruct((m, d), BF16)],
        compiler_params=_cparams(("parallel",)),
        name="resid_norm",
    )(x, f, g.reshape(1, d), gn.reshape(1, d))


def _head_allsum(p, heads):
    q = p + pltpu.roll(p, 2 * heads, axis=1)
    return q + pltpu.roll(q, heads, axis=1)


def _col_slices(x):
    return [x[:, i * LANES:(i + 1) * LANES] for i in range(x.shape[1] // LANES)]


def _split2(x):
    hi = x.astype(BF16)
    return hi, (x - hi.astype(F32)).astype(BF16)


def _group_select(heads):
    src = np.arange(LANES)[:, None]
    dst = np.arange(2 * LANES)[None, :]
    mats = [(src == (2 * p + dst // LANES) * heads + dst % heads) for p in range(2)]
    return jnp.asarray(np.stack(mats), BF16)


def _prep_body(fm_ref, fl_ref, pm_ref, plo_ref, spm_ref, spl_ref, mum_ref, mul_ref, w0_ref, w2_ref,
               a0_ref, a2_ref, g2_ref, kk_ref, ka_ref, rk_ref, sel_ref,
               e_nkk, e_d, e_b, e_k, e_r, v_out, bonus_out, g_out, *, seq_shift, heads):
    width = heads * RWKV_HEAD

    def shifted(x, p_ref, sp_ref):
        if not seq_shift:
            return p_ref[...]
        before = jnp.where(pl.program_id(0) == 0, sp_ref[...], p_ref[7:8, :])
        row = lax.broadcasted_iota(jnp.int32, x.shape, 0)
        return jnp.where(row == 0, before, pltpu.roll(x, 1, axis=0))

    x = fm_ref[...]
    xm = x + (shifted(x, pm_ref, spm_ref) - x) * mum_ref[...]
    lo = fl_ref[...]
    lm = lo + (shifted(lo, plo_ref, spl_ref) - lo) * mul_ref[...]

    r = xm[:, :width]
    k = xm[:, width:2 * width]
    v = xm[:, 2 * width:]

    u = w0_ref[...] + jnp.dot(jnp.tanh(lm[:, :DECAY_LORA]).astype(BF16), w2_ref[...],
                              preferred_element_type=F32)
    w_log = -(jnp.maximum(-u, 0.0) + jnp.log1p(jnp.exp(-jnp.abs(u)))) - 0.5
    decay = jnp.exp(-jnp.exp(w_log))
    a = jax.nn.sigmoid(a0_ref[...] + jnp.dot(lm[:, DECAY_LORA:2 * DECAY_LORA].astype(BF16),
                                             a2_ref[...], preferred_element_type=F32))
    g = jnp.dot(jax.nn.sigmoid(lm[:, DECAY_LORA:]).astype(BF16), g2_ref[...],
                preferred_element_type=F32)

    kk = k * kk_ref[...]
    sq = _col_slices(kk * kk)
    tot = sq[0]
    for s in sq[1:]:
        tot = tot + s
    den = jnp.maximum(jnp.sqrt(_head_allsum(tot, heads)), L2_EPS)
    kk = jnp.concatenate([c / den for c in _col_slices(kk)], axis=1)
    kf = k * (1.0 + (a - 1.0) * ka_ref[...])
    b = kk * a

    rk_parts = _col_slices(r * kf * rk_ref[...])
    rk_tot = rk_parts[0]
    for s in rk_parts[1:]:
        rk_tot = rk_tot + s
    rk_tot = _head_allsum(rk_tot, heads)
    v_out[...] = v
    bonus_out[...] = jnp.concatenate([rk_tot * c for c in _col_slices(v)], axis=1)
    g_out[...] = g

    rows = x.shape[0]

    def expand(val, ref):
        hi, lo = _split2(jnp.concatenate(_col_slices(val), axis=0))
        for p in range(2):
            sel = sel_ref[p]
            out = (jnp.dot(hi, sel, preferred_element_type=F32)
                   + jnp.dot(lo, sel, preferred_element_type=F32))
            for i in range(width // LANES):
                for q in range(2):
                    ref[4 * i + 2 * p + q] = out[i * rows:(i + 1) * rows, q * LANES:(q + 1) * LANES]

    expand(-kk, e_nkk)
    expand(decay, e_d)
    expand(b, e_b)
    expand(kf, e_k)
    expand(r, e_r)


def _rwkv_prep(feats, lora, shift_m, shift_l, prm, *, seq_shift, heads, tm):
    m = feats.shape[0]
    width = heads * RWKV_HEAD
    tm = _tile(m, tm)
    nt = m // tm
    if seq_shift:
        assert tm % 8 == 0
        prev_m, prev_l = feats, lora
        pm_spec = pl.BlockSpec((8, 3 * width), lambda i: (jnp.maximum(i * (tm // 8) - 1, 0), 0))
        pl_spec = pl.BlockSpec((8, LORA_PAD), lambda i: (jnp.maximum(i * (tm // 8) - 1, 0), 0))
    else:
        prev_m, prev_l = shift_m, shift_l
        pm_spec = pl.BlockSpec((tm, 3 * width), lambda i: (i, 0))
        pl_spec = pl.BlockSpec((tm, LORA_PAD), lambda i: (i, 0))

    def vec(n):
        return pl.BlockSpec((1, n), lambda i: (0, 0))

    def full(r_, c_):
        return pl.BlockSpec((r_, c_), lambda i: (0, 0))

    e_spec = pl.BlockSpec((RWKV_HEAD, tm, LANES), lambda i: (0, i, 0))
    c_spec = pl.BlockSpec((tm, width), lambda i: (i, 0))
    e_shape = jax.ShapeDtypeStruct((RWKV_HEAD, m, LANES), F32)
    c_shape = jax.ShapeDtypeStruct((m, width), F32)
    return pl.pallas_call(
        functools.partial(_prep_body, seq_shift=seq_shift, heads=heads),
        grid=(nt,),
        in_specs=[pl.BlockSpec((tm, 3 * width), lambda i: (i, 0)),
                  pl.BlockSpec((tm, LORA_PAD), lambda i: (i, 0)),
                  pm_spec, pl_spec, vec(3 * width), vec(LORA_PAD), vec(3 * width), vec(LORA_PAD),
                  vec(width), full(DECAY_LORA, width), vec(width), full(DECAY_LORA, width),
                  full(LORA_PAD - DECAY_LORA, width), vec(width), vec(width), vec(width),
                  pl.BlockSpec((2, LANES, 2 * LANES), lambda i: (0, 0, 0))],
        out_specs=[e_spec] * 5 + [c_spec] * 3,
        out_shape=[e_shape] * 5 + [c_shape] * 3,
        compiler_params=_cparams(("parallel",)),
        name="rwkv_prep",
    )(feats, lora, prev_m, prev_l, shift_m[:1], shift_l[:1], prm["mu_m"], prm["mu_l"], prm["w0"],
      prm["w2"], prm["a0"], prm["a2"], prm["g2"], prm["k_k"], prm["k_a"], prm["r_k"],
      _group_select(heads))


N_ACC = 4


def _tree_sum(parts):
    while len(parts) > 1:
        parts = [parts[i] + parts[i + 1] for i in range(0, len(parts), 2)]
    return parts[0]


def _wkv_sa(s_ref, nkk_ref, t):
    acc = [None] * N_ACC
    for kx in range(RWKV_HEAD):
        term = s_ref[kx] * nkk_ref[kx, pl.ds(t, 1), :]
        acc[kx % N_ACC] = term if acc[kx % N_ACC] is None else acc[kx % N_ACC] + term
    return _tree_sum(acc)


def _wkv_step(s_ref, nkk_ref, d_ref, b_ref, k_ref, r_ref, v_t, sa, t, t_next):
    yacc = [None] * N_ACC
    sacc = [None] * N_ACC
    for kx in range(RWKV_HEAD):
        s_new = (s_ref[kx] * d_ref[kx, pl.ds(t, 1), :] + sa * b_ref[kx, pl.ds(t, 1), :]
                 + v_t * k_ref[kx, pl.ds(t, 1), :])
        s_ref[kx] = s_new
        term = s_new * r_ref[kx, pl.ds(t, 1), :]
        yacc[kx % N_ACC] = term if yacc[kx % N_ACC] is None else yacc[kx % N_ACC] + term
        if t_next is not None:
            term = s_new * nkk_ref[kx, pl.ds(t_next, 1), :]
            sacc[kx % N_ACC] = term if sacc[kx % N_ACC] is None else sacc[kx % N_ACC] + term
    return _tree_sum(yacc), (None if t_next is None else _tree_sum(sacc))


def _scan_seq_body(s0_ref, nkk_ref, d_ref, b_ref, k_ref, r_ref, v_ref, y_ref, sout_ref, s_ref, *,
                   tc):
    c = pl.program_id(0)

    @pl.when(c == 0)
    def _():
        s_ref[...] = s0_ref[...]

    def step(t, sa):
        y_ref[t], sa_next = _wkv_step(s_ref, nkk_ref, d_ref, b_ref, k_ref, r_ref, v_ref[t], sa, t,
                                      jnp.minimum(t + 1, tc - 1))
        return sa_next

    lax.fori_loop(0, tc, step, _wkv_sa(s_ref, nkk_ref, 0), unroll=16)

    @pl.when(c == pl.num_programs(0) - 1)
    def _():
        sout_ref[...] = s_ref[...]


def _wkv_scan_seq(s0, e_list, v, *, tc=64):
    t_len = v.shape[0]
    tc = _tile(t_len, tc)
    s_spec = pl.BlockSpec(s0.shape, lambda c: (0, 0, 0))
    e_spec = pl.BlockSpec((RWKV_HEAD, tc, LANES), lambda c: (0, c, 0))
    v_spec = pl.BlockSpec((tc,) + v.shape[1:], lambda c: (c, 0, 0))
    return pl.pallas_call(
        functools.partial(_scan_seq_body, tc=tc),
        grid=(t_len // tc,),
        in_specs=[s_spec] + [e_spec] * 5 + [v_spec],
        out_specs=[v_spec, s_spec],
        out_shape=[jax.ShapeDtypeStruct(v.shape, F32), jax.ShapeDtypeStruct(s0.shape, F32)],
        scratch_shapes=[pltpu.VMEM(s0.shape, F32)],
        compiler_params=_cparams(("arbitrary",)),
        name="wkv_scan",
    )(s0, *e_list, v)


def _scan_batch_body(s0_ref, nkk_ref, d_ref, b_ref, k_ref, r_ref, v_ref, y_ref, sout_ref):
    bidx = pl.program_id(0)
    s = sout_ref.at[0]
    s[...] = s0_ref[0]
    y_ref[0], _ = _wkv_step(s, nkk_ref, d_ref, b_ref, k_ref, r_ref, v_ref[0],
                            _wkv_sa(s, nkk_ref, bidx), bidx, None)


def _wkv_scan_batch(s0, e_list, v):
    nb = v.shape[0]
    s_spec = pl.BlockSpec((1,) + s0.shape[1:], lambda b: (b, 0, 0, 0))
    e_spec = pl.BlockSpec((RWKV_HEAD, nb, LANES), lambda b: (0, 0, 0))
    v_spec = pl.BlockSpec((1,) + v.shape[1:], lambda b: (b, 0, 0))
    return pl.pallas_call(
        _scan_batch_body,
        grid=(nb,),
        in_specs=[s_spec] + [e_spec] * 5 + [v_spec],
        out_specs=[v_spec, s_spec],
        out_shape=[jax.ShapeDtypeStruct(v.shape, F32), jax.ShapeDtypeStruct(s0.shape, F32)],
        compiler_params=_cparams(("arbitrary",)),
        name="wkv_step",
    )(s0, *e_list, v)


def _post_body(y_ref, bonus_ref, g_ref, lnw_ref, lnb_ref, o_ref, *, heads):
    inv_n = 1.0 / RWKV_HEAD

    def head_sum(x):
        cols = _col_slices(x)
        tot = cols[0]
        for c in cols[1:]:
            tot = tot + c
        return _head_allsum(tot, heads)

    def tiled(stat, like):
        return jnp.concatenate([stat] * (like.shape[1] // LANES), axis=1)

    y = y_ref[...]
    yc = y - tiled(head_sum(y) * inv_n, y)
    var = head_sum(yc * yc) * inv_n
    yn = yc * tiled(lax.rsqrt(var + GN_EPS), y) * lnw_ref[...] + lnb_ref[...]
    o_ref[...] = ((yn + bonus_ref[...]) * g_ref[...]).astype(o_ref.dtype)


def _rwkv_post(y, bonus, g, prm, *, heads, tm=256):
    m, width = y.shape
    tm = _tile(m, tm)
    row = pl.BlockSpec((tm, width), lambda i: (i, 0))
    vec = pl.BlockSpec((1, width), lambda i: (0, 0))
    return pl.pallas_call(
        functools.partial(_post_body, heads=heads),
        grid=(m // tm,),
        in_specs=[row] * 3 + [vec] * 2,
        out_specs=row,
        out_shape=jax.ShapeDtypeStruct((m, width), BF16),
        compiler_params=_cparams(("parallel",)),
        name="rwkv_post",
    )(y, bonus, g, prm["ln_w"], prm["ln_b"])


SB_TILE = 512
SB_SUB = 256
SB_HEADS_PER_STEP = 8
LOG2E = 1.4426950408889634


def _log_sigmoid_pair(z):
    lsz = jnp.minimum(z, 0.0) - jnp.log(1.0 + jnp.exp2(jnp.abs(z) * (-LOG2E)))
    return lsz, lsz - z


def _sb_weights(lsz, lsn, n_blocks, tri2_ref, carry, valid):
    rows = lsz.shape[0] // n_blocks
    hi = lsn.astype(BF16)
    lo = (lsn - hi.astype(F32)).astype(BF16)
    if n_blocks == 1:
        tri = tri2_ref[:lsz.shape[1]]
        logw = lsz + (jnp.dot(hi, tri, preferred_element_type=F32)
                      + jnp.dot(lo, tri, preferred_element_type=F32))
    else:
        logw = lsz + jnp.dot(jnp.concatenate([hi, lo], axis=1), tri2_ref[...],
                             preferred_element_type=F32)
    taken = jnp.sum(lsn, axis=1, keepdims=True)
    weights = []
    for b in range(n_blocks):
        sl = slice(b * rows, (b + 1) * rows)
        w = jnp.exp(logw[sl] + carry)
        if valid is not None:
            w = jnp.where(valid[sl], w, 0.0)
        weights.append(w.astype(BF16))
        carry = carry + taken[sl]
    return weights, carry


def _tri2(n):
    tri = (lax.broadcasted_iota(jnp.int32, (n, n), 0)
           > lax.broadcasted_iota(jnp.int32, (n, n), 1)).astype(BF16)
    return jnp.concatenate([tri, tri], axis=0)


def _suffix_sum(ls, tri_ref):
    hi = ls.astype(BF16)
    lo = (ls - hi.astype(F32)).astype(BF16)
    tri = tri_ref[...]
    return jnp.dot(jnp.concatenate([hi, lo], axis=1), jnp.concatenate([tri, tri], axis=0),
                   preferred_element_type=F32)


def _sbp_body(qi_ref, kb_ref, q_ref, k_ref, v_ref, tri_ref, g_ref, bias_ref, ones_ref,
              o_ref, qs_ref, acc_ref, carry_ref):
    s = pl.program_id(1)
    tile = q_ref.shape[0]
    n_sub = tri_ref.shape[0]
    n_heads = q_ref.shape[1] // SB_HEAD
    diagonal = kb_ref[s] == qi_ref[s]

    def visit(masked):
        if masked:
            causal = (lax.broadcasted_iota(jnp.int32, (tile, tile), 1)
                      < lax.broadcasted_iota(jnp.int32, (tile, tile), 0))
        for hh in range(n_heads):
            cols = slice(hh * SB_HEAD, (hh + 1) * SB_HEAD)
            kmat = jnp.concatenate([k_ref[:, cols].astype(BF16), ones_ref[...]], axis=1)
            vmat = v_ref[:, cols].astype(BF16)
            qext = qs_ref[:, 2 * hh * SB_HEAD:2 * (hh + 1) * SB_HEAD]
            carry = carry_ref[hh]
            w_parts = [None] * (tile // n_sub)
            for sub in reversed(range(tile // n_sub)):
                sl = slice(sub * n_sub, (sub + 1) * n_sub)
                z = lax.dot_general(qext, kmat[sl], (((1,), (1,)), ((), ())),
                                    preferred_element_type=F32)
                lsz, lsn = _log_sigmoid_pair(z)
                if masked:
                    lsn = jnp.where(causal[:, sl], lsn, 0.0)
                w = jnp.exp(lsz + _suffix_sum(lsn, tri_ref) + carry)
                if masked:
                    w = jnp.where(causal[:, sl], w, 0.0)
                w_parts[sub] = w.astype(BF16)
                carry = carry + jnp.sum(lsn, axis=1, keepdims=True)
            carry_ref[hh] = carry
            acc_ref[:, cols] += jnp.dot(jnp.concatenate(w_parts, axis=1), vmat,
                                        preferred_element_type=F32)

    @pl.when(diagonal)
    def _():
        for hh in range(n_heads):
            cols = slice(hh * SB_HEAD, (hh + 1) * SB_HEAD)
            qs_ref[:, 2 * hh * SB_HEAD:(2 * hh + 1) * SB_HEAD] = (
                q_ref[:, cols] * (SB_HEAD ** -0.5)).astype(BF16)
            qs_ref[:, (2 * hh + 1) * SB_HEAD:2 * (hh + 1) * SB_HEAD] = jnp.broadcast_to(
                bias_ref[hh], (tile, SB_HEAD))
        acc_ref[...] = jnp.zeros_like(acc_ref)
        carry_ref[...] = jnp.zeros_like(carry_ref)
        visit(True)

    @pl.when(jnp.logical_not(diagonal))
    def _():
        visit(False)

    @pl.when(kb_ref[s] == 0)
    def _():
        for hh in range(n_heads):
            cols = slice(hh * SB_HEAD, (hh + 1) * SB_HEAD)
            o_ref[:, cols] = _rms(acc_ref[:, cols], g_ref[hh]).astype(o_ref.dtype)


def _sb_schedule(n_tiles):
    qi = [i for i in range(n_tiles) for _ in range(i + 1)]
    kb = [j for i in range(n_tiles) for j in range(i, -1, -1)]
    return [jnp.asarray(np.asarray(a, np.int32)) for a in (qi, kb)]


def _sb_prompt(qkv, q_col, bias, gain, *, heads):
    t_len = qkv.shape[0]
    tile = _tile(t_len, SB_TILE)
    sched = _sb_schedule(t_len // tile)
    nsteps = sched[0].shape[0]
    sub = min(SB_SUB, tile)
    hps = SB_HEADS_PER_STEP
    assert heads % hps == 0 and q_col % hps == 0
    wide = hps * SB_HEAD
    qc = q_col // hps
    b_hi = bias.astype(BF16)
    b_lo = (bias - b_hi.astype(F32)).astype(BF16)
    lane = lax.broadcasted_iota(jnp.int32, (heads, 1, SB_HEAD), 2)
    bias_ext = jnp.where(lane == 0, b_hi[:, None, None],
                         jnp.where(lane == 1, b_lo[:, None, None], jnp.zeros((), BF16)))
    ones_ext = (lax.broadcasted_iota(jnp.int32, (tile, SB_HEAD), 1) < 2).astype(BF16)
    grid_spec = pltpu.PrefetchScalarGridSpec(
        num_scalar_prefetch=2,
        grid=(heads // hps, nsteps),
        in_specs=[
            pl.BlockSpec((tile, wide), lambda h, s, qi, kb: (qi[s], qc + h)),
            pl.BlockSpec((tile, wide), lambda h, s, qi, kb: (kb[s], qc + heads // hps + h)),
            pl.BlockSpec((tile, wide), lambda h, s, qi, kb: (kb[s], qc + 2 * (heads // hps) + h)),
            pl.BlockSpec((sub, sub), lambda h, s, qi, kb: (0, 0)),
            pl.BlockSpec((hps, 1, SB_HEAD), lambda h, s, qi, kb: (h, 0, 0)),
            pl.BlockSpec((hps, 1, SB_HEAD), lambda h, s, qi, kb: (h, 0, 0)),
            pl.BlockSpec((tile, SB_HEAD), lambda h, s, qi, kb: (0, 0)),
        ],
        out_specs=pl.BlockSpec((tile, wide), lambda h, s, qi, kb: (qi[s], h)),
        scratch_shapes=[pltpu.VMEM((tile, 2 * wide), BF16), pltpu.VMEM((tile, wide), F32),
                        pltpu.VMEM((hps, tile, 1), F32)],
    )
    return pl.pallas_call(
        _sbp_body,
        grid_spec=grid_spec,
        out_shape=jax.ShapeDtypeStruct((t_len, heads * SB_HEAD), BF16),
        compiler_params=_cparams(("parallel", "arbitrary")),
        name="sb_prompt",
    )(*sched, qkv, qkv, qkv, _tri2(sub)[:sub], gain.reshape(heads, 1, SB_HEAD), bias_ext, ones_ext)


SBS_PAGES = 8


def _sbs_body(pt_ref, q_ref, *refs, heads, pages):
    k_refs, v_refs = refs[:pages], refs[pages:2 * pages]
    tri_ref, bias_ref, g_ref, o_ref, qs_ref, acc_ref, carry_ref = refs[2 * pages:]
    p = pl.program_id(1)
    rows = k_refs[0].shape[1] * heads
    n_sub = tri_ref.shape[1]
    subs = list(reversed(range(rows // n_sub)))

    @pl.when(p == 0)
    def _():
        qs_ref[...] = (q_ref[0] * (SB_HEAD ** -0.5)).astype(BF16)
        acc_ref[...] = jnp.zeros_like(acc_ref)
        carry_ref[...] = jnp.zeros_like(carry_ref)

    z2_blocks = []
    for k_ref in k_refs:
        kmat = k_ref[0].reshape(rows, SB_HEAD).astype(BF16)
        z2 = lax.dot_general(qs_ref[...], kmat, (((1,), (1,)), ((), ())),
                             preferred_element_type=F32) + bias_ref[...]
        z2_blocks += [z2[:, u * n_sub:(u + 1) * n_sub] for u in subs]
    stacked = (len(z2_blocks) * heads, n_sub)
    own_head = (lax.broadcasted_iota(jnp.int32, stacked, 1) % heads
                == lax.broadcasted_iota(jnp.int32, stacked, 0) % heads)
    lsz, lsn = _log_sigmoid_pair(jnp.concatenate(z2_blocks, axis=0))
    weights, carry_ref[...] = _sb_weights(lsz, jnp.where(own_head, lsn, 0.0), len(z2_blocks),
                                          tri_ref, carry_ref[...], own_head)
    acc = acc_ref[...]
    for u, v_ref in enumerate(v_refs):
        vmat = v_ref[0].reshape(rows, SB_HEAD).astype(BF16)
        w_page = weights[u * len(subs):(u + 1) * len(subs)][::-1]
        acc = acc + jnp.dot(jnp.concatenate(w_page, axis=1), vmat, preferred_element_type=F32)
    acc_ref[...] = acc

    @pl.when(p == pl.num_programs(1) - 1)
    def _():
        o_ref[0] = _rms(acc_ref[...], g_ref[...]).astype(o_ref.dtype)


def _sb_sample(q, cache_k, cache_v, page_table, bias, gain, *, heads):
    nb = q.shape[0]
    n_pages = page_table.shape[1]
    page = cache_k.shape[1]
    sub = min(SB_SUB, page * heads)
    pages = SBS_PAGES
    while n_pages % pages:
        pages //= 2

    def kv_spec(u):
        return pl.BlockSpec((1, page, heads, SB_HEAD),
                            lambda b, p, pt: (pt[b, n_pages - 1 - (p * pages + u)], 0, 0, 0))

    kv_specs = [kv_spec(u) for u in range(pages)]
    grid_spec = pltpu.PrefetchScalarGridSpec(
        num_scalar_prefetch=1,
        grid=(nb, n_pages // pages),
        in_specs=[pl.BlockSpec((1, heads, SB_HEAD), lambda b, p, pt: (b, 0, 0))]
                 + kv_specs + kv_specs
                 + [pl.BlockSpec((2 * sub, sub), lambda b, p, pt: (0, 0)),
                    pl.BlockSpec((heads, 1), lambda b, p, pt: (0, 0)),
                    pl.BlockSpec((heads, SB_HEAD), lambda b, p, pt: (0, 0))],
        out_specs=pl.BlockSpec((1, heads, SB_HEAD), lambda b, p, pt: (b, 0, 0)),
        scratch_shapes=[pltpu.VMEM((heads, SB_HEAD), BF16), pltpu.VMEM((heads, SB_HEAD), F32),
                        pltpu.VMEM((heads, 1), F32)],
    )
    return pl.pallas_call(
        functools.partial(_sbs_body, heads=heads, pages=pages),
        grid_spec=grid_spec,
        out_shape=jax.ShapeDtypeStruct((nb, heads, SB_HEAD), BF16),
        compiler_params=_cparams(("parallel", "arbitrary")),
        name="sb_sample",
    )(page_table, q, *([cache_k] * pages), *([cache_v] * pages), _tri2(sub),
      bias.reshape(heads, 1), gain)


def _perm_cols(w, heads):
    lead = w.shape[:-1]
    return w.reshape(lead + (heads, RWKV_HEAD)).swapaxes(-1, -2).reshape(lead + (heads * RWKV_HEAD,))


def _unperm_cols(w, heads):
    lead = w.shape[:-1]
    return w.reshape(lead + (RWKV_HEAD, heads)).swapaxes(-1, -2).reshape(lead + (heads * RWKV_HEAD,))


def _state_in(s):
    heads = s.shape[-3]
    lead = s.shape[:-3]
    s = jnp.moveaxis(s, -3, -1)
    s = jnp.swapaxes(s, -3, -2)
    return s.reshape(lead + (RWKV_HEAD, RWKV_HEAD * heads // LANES, LANES))


def _state_out(s, heads):
    lead = s.shape[:-3]
    s = s.reshape(lead + (RWKV_HEAD, RWKV_HEAD, heads))
    s = jnp.swapaxes(s, -3, -2)
    return jnp.moveaxis(s, -1, -3)


def _layer(x, pe, shift_prev, wkv0, attend, wts, *, seq_shift):
    heads = wts["heads"]
    width = heads * RWKV_HEAD
    sb_heads = wts["sb_heads"]
    sb_width = sb_heads * SB_HEAD
    m = x.shape[0]

    n_lora = DECAY_LORA + AAA_LORA + GATE_LORA
    h = _norm_cast(x, wts["norm_mix_pre"])
    feats = _matmul(h, wts["w_rkv"], name="mm_in")
    qkv = _matmul(h, wts["w_qkv"], name="mm_qkv")
    lora = _matmul(h, wts["w_lora"], name="mm_lora")

    rows = slice(m - 1, m) if seq_shift else slice(0, m)
    shift_new = jnp.concatenate(
        [_unperm_cols(feats[rows].reshape(-1, 3, width), heads).reshape(-1, 3 * width),
         lora[rows, :n_lora]], axis=1)
    tm = _tile(m, 64) if seq_shift else m
    outs = _rwkv_prep(feats, lora, shift_prev[:, :3 * width], shift_prev[:, 3 * width:], wts,
                      seq_shift=seq_shift, heads=heads, tm=tm)
    e_list, (v_c, bonus_c, g_c) = outs[:5], outs[5:]
    v_rows = v_c.reshape(m, width // LANES, LANES)
    if seq_shift:
        y, s_fin = _wkv_scan_seq(_state_in(wkv0[0]), e_list, v_rows)
        wkv_new = _state_out(s_fin, heads)[None]
    else:
        y, s_fin = _wkv_scan_batch(_state_in(wkv0), e_list, v_rows)
        wkv_new = _state_out(s_fin, heads)
    r_out = _rwkv_post(y.reshape(m, width), bonus_c, g_c, wts, heads=heads)

    a_out = attend(qkv)
    kh = qkv[:, sb_width:2 * sb_width].reshape(m, sb_heads, SB_HEAD)
    vh = qkv[:, 2 * sb_width:].reshape(m, sb_heads, SB_HEAD)

    mix = _matmul_cat(r_out, a_out, wts["w_out_r"], wts["w_out_a"])
    x1, h2 = _resid_norm(x, mix, wts["norm_mix_post"], wts["norm_ffn_pre"])
    up = _matmul(h2, wts["w_up"], out_dtype=BF16, relu2=True, name="mm_up")
    f = _matmul(up, wts["w_down"], tk=4096, name="mm_down")
    x2, x2b = _resid_norm(x1, f, wts["norm_ffn_post"], None)
    out = _ple(x2b, wts["w_ple_gate"], x2, pe.astype(BF16), wts["w_ple_proj"])
    return out, shift_new, wkv_new, kh, vh


def _perm3(w, width, heads):
    lead = w.shape[:-1]
    return _perm_cols(w.reshape(lead + (3, width)), heads).reshape(lead + (3 * width,))


def _shift_in(s, width, heads):
    pad = LORA_PAD - (s.shape[1] - 3 * width)
    return jnp.concatenate([_perm3(s[:, :3 * width], width, heads), s[:, 3 * width:],
                            jnp.zeros((s.shape[0], pad), s.dtype)], axis=1)


def kernel(x_prompt, x_sample, p_prompt, p_sample, state_rwkv_shift, state_rwkv_wkv, cache_k, cache_v, page_table, norm_mix_pre, norm_mix_post, norm_ffn_pre, norm_ffn_post, w_in, rwkv_mu, rwkv_w0, rwkv_w2, rwkv_a0, rwkv_a2, rwkv_g2, rwkv_k_k, rwkv_k_a, rwkv_r_k, rwkv_ln_w, rwkv_ln_b, sb_norm, sb_bias, w_out, w_up, w_down, w_ple_gate, w_ple_proj):
    depth = w_in.shape[0]
    heads = rwkv_r_k.shape[1]
    width = heads * RWKV_HEAD
    sb_heads = sb_bias.shape[1]
    sb_width = sb_heads * SB_HEAD
    nb, t_len, d_model = x_prompt.shape
    db = x_sample.shape[0]
    assert nb == 1 and x_sample.shape[1] == 1 and LANES == 4 * heads
    rwkv_proj = rwkv_mu.shape[1]
    n_lora = rwkv_proj - 3 * width
    assert n_lora == DECAY_LORA + AAA_LORA + GATE_LORA and sb_width + width == w_out.shape[1]

    yp, ys = x_prompt[0], x_sample[:, 0]
    res = [[] for _ in range(8)]
    for i in range(depth):
        wi = w_in[i]
        lpad = jnp.zeros((d_model, LORA_PAD - n_lora), wi.dtype)
        mu = rwkv_mu[i]
        a2p = jnp.concatenate([rwkv_a2[i], jnp.zeros((DECAY_LORA - AAA_LORA, width), F32)], axis=0)
        g2p = jnp.concatenate([jnp.zeros((AAA_LORA, width), F32), rwkv_g2[i],
                               jnp.zeros((LORA_PAD - n_lora, width), F32)], axis=0)
        wo = w_out[i]
        wts = dict(
            heads=heads, sb_heads=sb_heads,
            norm_mix_pre=norm_mix_pre[i], norm_mix_post=norm_mix_post[i],
            norm_ffn_pre=norm_ffn_pre[i], norm_ffn_post=norm_ffn_post[i],
            w_rkv=_perm3(wi[:, :3 * width], width, heads).astype(BF16),
            w_qkv=wi[:, rwkv_proj:].astype(BF16),
            w_lora=jnp.concatenate([wi[:, 3 * width:rwkv_proj], lpad], axis=1).astype(BF16),
            mu_m=_perm3(mu[:3 * width], width, heads).reshape(1, -1),
            mu_l=jnp.concatenate([mu[3 * width:], jnp.zeros((LORA_PAD - n_lora,), F32)]).reshape(1, -1),
            w0=_perm_cols(rwkv_w0[i], heads).reshape(1, -1),
            w2=_perm_cols(rwkv_w2[i], heads).astype(BF16),
            a0=_perm_cols(rwkv_a0[i], heads).reshape(1, -1),
            a2=_perm_cols(a2p, heads).astype(BF16),
            g2=_perm_cols(g2p, heads).astype(BF16),
            k_k=_perm_cols(rwkv_k_k[i], heads).reshape(1, -1),
            k_a=_perm_cols(rwkv_k_a[i], heads).reshape(1, -1),
            r_k=_perm_cols(rwkv_r_k[i].reshape(-1), heads).reshape(1, -1),
            ln_w=_perm_cols(rwkv_ln_w[i], heads).reshape(1, -1),
            ln_b=_perm_cols(rwkv_ln_b[i], heads).reshape(1, -1),
            w_out_r=wo[:width].astype(BF16).reshape(heads, RWKV_HEAD, -1).swapaxes(0, 1)
                    .reshape(width, -1),
            w_out_a=wo[width:].astype(BF16),
            w_up=w_up[i].astype(BF16), w_down=w_down[i].astype(BF16),
            w_ple_gate=w_ple_gate[i].astype(BF16), w_ple_proj=w_ple_proj[i].astype(BF16),
        )
        bias, gain = sb_bias[i], sb_norm[i]

        attend_p = lambda qkv: _sb_prompt(qkv, 0, bias, gain, heads=sb_heads)
        zero_shift = _shift_in(jnp.zeros((1, rwkv_proj), F32), width, heads)
        zero_wkv = jnp.zeros((1, heads, RWKV_HEAD, RWKV_HEAD), F32)
        yp, sp, wp, kp, vp = _layer(yp, p_prompt[i, 0], zero_shift, zero_wkv, attend_p, wts,
                                    seq_shift=True)

        n_phys = cache_k.shape[1]
        ck = cache_k.reshape((depth * n_phys,) + cache_k.shape[2:])
        cv = cache_v.reshape((depth * n_phys,) + cache_v.shape[2:])
        pages_i = page_table + i * n_phys
        attend_s = lambda qkv: _sb_sample(
            qkv[:, :sb_width].reshape(db, sb_heads, SB_HEAD), ck, cv, pages_i, bias, gain,
            heads=sb_heads).reshape(db, sb_width)
        ys, ss, ws, kn, vn = _layer(ys, p_sample[i, :, 0], _shift_in(state_rwkv_shift[i], width, heads),
                                    state_rwkv_wkv[i], attend_s, wts, seq_shift=False)
        for lst, val in zip(res, (sp, ss, wp, ws, kp[None], vp[None], kn[:, None], vn[:, None])):
            lst.append(val)
    return (yp[None], ys[:, None]) + tuple(jnp.stack(r) for r in res)
```
